```python
import math
import jax, jax.numpy as jnp
from jax import lax
import numpy as np

D_MODEL = 1024
BATCH = 4
SEQ = 4096
DEPTH = 1
DEC_BATCH = 128
DEC_SEQ = 4
PAST_LEN = 8192
PAGE_SIZE = 128

H_DIFF = D_MODEL // 256
DH_DIFF = 64
DV_DIFF = 2 * DH_DIFF
H_SB = D_MODEL // 128
DH_SB = 64
QK_DIFF = 2 * H_DIFF * DH_DIFF
V_DIFF = H_DIFF * DV_DIFF
W_SB = H_SB * DH_SB
MIX_WIDTH = V_DIFF + W_SB
IN_COLS = 2 * QK_DIFF + V_DIFF + 3 * W_SB
SPLITS = (QK_DIFF, 2 * QK_DIFF, 2 * QK_DIFF + V_DIFF, 2 * QK_DIFF + V_DIFF + W_SB,
          2 * QK_DIFF + V_DIFF + 2 * W_SB)
ROPE_THETA = 10000.0
Q_BLOCK = 128
N_GROUPS = 4
EXPERTS_PER_GROUP = 8
N_EXPERTS = N_GROUPS * EXPERTS_PER_GROUP
TOP_K = 2
EXPERT_FF = D_MODEL // 2
MOE_BLOCK = 128
NORM_EPS = 1e-6

kernel_name = 'hymba_diff_stickbreak_hmoe_step'


def rms_norm(x, g):
    xf = x.astype(jnp.float32)
    y = xf * lax.rsqrt(jnp.mean(xf * xf, axis=-1, keepdims=True) + NORM_EPS)
    return (y * g.astype(jnp.float32)).astype(x.dtype)


def rope(x, pos):
    dh = x.shape[-1]
    half = dh // 2
    inv = jnp.power(jnp.float32(ROPE_THETA), -(jnp.arange(half, dtype=jnp.float32) * 2.0 / dh))
    ang = pos.astype(jnp.float32)[:, None] * inv[None, :]
    cos = jnp.cos(ang)[None, :, None, :]
    sin = jnp.sin(ang)[None, :, None, :]
    xf = x.astype(jnp.float32)
    x1, x2 = xf[..., :half], xf[..., half:]
    return jnp.concatenate([x1 * cos - x2 * sin, x2 * cos + x1 * sin], axis=-1).astype(x.dtype)


def attn_inputs(x, pos, g_attn, w_in, g_qn, g_kn):
    b, t, _ = x.shape
    proj = rms_norm(x, g_attn) @ w_in
    qd, kd, vd, qs, ks, vs = jnp.split(proj, SPLITS, axis=-1)
    qd = rope(rms_norm(qd.reshape(b, t, 2 * H_DIFF, DH_DIFF), g_qn), pos)
    kd = rope(rms_norm(kd.reshape(b, t, 2 * H_DIFF, DH_DIFF), g_kn), pos)
    vd = vd.reshape(b, t, H_DIFF, DV_DIFF)
    qs = qs.reshape(b, t, H_SB, DH_SB)
    ks = ks.reshape(b, t, H_SB, DH_SB)
    vs = vs.reshape(b, t, H_SB, DH_SB)
    return qd, kd, vd, qs, ks, vs


def diff_attn_core(q, k, v, q_pos, k_pos, lam, g_subln, lam_init):
    b, tq, n_sub, dh = q.shape
    s = jnp.einsum('bqnd,bknd->bnqk', q, k, preferred_element_type=jnp.float32) * (dh ** -0.5)
    causal = k_pos[None, :] <= q_pos[:, None]
    p = jax.nn.softmax(jnp.where(causal, s, -jnp.inf), axis=-1)
    p = p.reshape(b, n_sub // 2, 2, tq, k.shape[1])
    a = p[:, :, 0] - lam * p[:, :, 1]
    o = jnp.einsum('bhqk,bkhe->bqhe', a, v, preferred_element_type=jnp.float32)
    return rms_norm(o, g_subln) * (1.0 - lam_init)


def sb_attn_core(q, k, v, q_pos, k_pos, g_out):
    dh = q.shape[-1]
    z = jnp.einsum('bqhd,bkhd->bhqk', q, k, preferred_element_type=jnp.float32) * (dh ** -0.5)
    valid = k_pos[None, :] < q_pos[:, None]
    log_1m = jnp.where(valid, jax.nn.log_sigmoid(-z), 0.0)
    shifted = jnp.pad(log_1m[..., 1:], ((0, 0), (0, 0), (0, 0), (0, 1)))
    later = lax.cumsum(shifted, axis=3, reverse=True)
    a = jnp.where(valid, jnp.exp(jax.nn.log_sigmoid(z) + later), 0.0)
    o = jnp.einsum('bhqk,bkhd->bqhd', a, v, preferred_element_type=jnp.float32)
    return rms_norm(o, g_out)


def prompt_attention(qd, kd, vd, qs, ks, vs, lam, g_subln, lam_init, g_sb):
    b, s = qd.shape[:2]
    k_pos = jnp.arange(s)

    def block(i):
        s0 = i * Q_BLOCK
        q_pos = s0 + jnp.arange(Q_BLOCK)
        od = diff_attn_core(lax.dynamic_slice_in_dim(qd, s0, Q_BLOCK, 1), kd, vd,
                            q_pos, k_pos, lam, g_subln, lam_init)
        osb = sb_attn_core(lax.dynamic_slice_in_dim(qs, s0, Q_BLOCK, 1), ks, vs,
                           q_pos, k_pos, g_sb)
        return od, osb

    od, osb = lax.map(block, jnp.arange(s // Q_BLOCK))
    od = jnp.moveaxis(od, 0, 1).reshape(b, s, H_DIFF, DV_DIFF)
    osb = jnp.moveaxis(osb, 0, 1).reshape(b, s, H_SB, DH_SB)
    return od, osb


def sample_attention(qd, kd, vd, qs, ks, vs, cache_kd, cache_vd, cache_ks, cache_vs,
                     page_table, layer, past_len, lam, g_subln, lam_init, g_sb):
    t = qd.shape[1]
    q_pos = past_len + jnp.arange(t)
    k_pos = jnp.arange(past_len + t)

    def one(args):
        qd_i, kd_i, vd_i, qs_i, ks_i, vs_i, pt_i = args

        def rows(cache, new):
            past = cache[layer, pt_i]
            past = past.reshape((-1,) + past.shape[2:])
            return jnp.concatenate([past, new], axis=0)[None]

        od = diff_attn_core(qd_i[None], rows(cache_kd, kd_i), rows(cache_vd, vd_i),
                            q_pos, k_pos, lam, g_subln, lam_init)[0]
        osb = sb_attn_core(qs_i[None], rows(cache_ks, ks_i), rows(cache_vs, vs_i),
                           q_pos, k_pos, g_sb)[0]
        return od, osb

    return lax.map(one, (qd, kd, vd, qs, ks, vs, page_table))


def route(xn, w_rg, b_rg, w_re, b_re):
    n = xn.shape[0]
    pg = jax.nn.softmax((xn @ w_rg).astype(jnp.float32) + b_rg, axis=-1)
    g = jnp.argmax(pg, axis=-1)
    p_top = jnp.take_along_axis(pg, g[:, None], axis=1)
    le = ((xn @ w_re).astype(jnp.float32) + b_re).reshape(n, N_GROUPS, EXPERTS_PER_GROUP)
    le = jnp.take_along_axis(le, g[:, None, None], axis=1)[:, 0]
    top_p, top_i = lax.top_k(jax.nn.softmax(le, axis=-1), TOP_K)
    gates = p_top * top_p / jnp.sum(top_p, axis=-1, keepdims=True)
    ids = (g[:, None] * EXPERTS_PER_GROUP + top_i).astype(jnp.int32)
    return ids, gates


def moe_experts(x, ids, gates, w_gate, w_up, w_down):
    n_tok, d = x.shape
    n_exp = w_gate.shape[0]
    n_assign = n_tok * ids.shape[1]
    flat_e = ids.reshape(-1)
    flat_t = jnp.repeat(jnp.arange(n_tok, dtype=jnp.int32), ids.shape[1])
    flat_g = gates.reshape(-1)
    order = jnp.argsort(flat_e)
    e_sorted, t_sorted, g_sorted = flat_e[order], flat_t[order], flat_g[order]
    counts = jnp.bincount(flat_e, length=n_exp)
    padded = (counts + MOE_BLOCK - 1) // MOE_BLOCK * MOE_BLOCK
    start = jnp.cumsum(counts) - counts
    pend = jnp.cumsum(padded)
    pstart = pend - padded
    dest = pstart[e_sorted] + (jnp.arange(n_assign) - start[e_sorted])
    n_slots = (-(-n_assign // MOE_BLOCK) + n_exp) * MOE_BLOCK
    n_blocks = n_slots // MOE_BLOCK
    slot_tok = jnp.full((n_slots,), n_tok, jnp.int32).at[dest].set(t_sorted)
    slot_gate = jnp.zeros((n_slots,), jnp.float32).at[dest].set(g_sorted)
    block_exp = jnp.minimum(jnp.searchsorted(pend, jnp.arange(n_blocks) * MOE_BLOCK, side='right'),
                            n_exp - 1)
    x_pad = jnp.concatenate([x, jnp.zeros((1, d), x.dtype)], axis=0)
    xb = x_pad[slot_tok].reshape(n_blocks, MOE_BLOCK, d)

    def run(args):
        xi, e = args
        h = jax.nn.silu(xi @ w_gate[e]) * (xi @ w_up[e])
        return h @ w_down[e]

    yb = lax.map(run, (xb, block_exp)).reshape(n_slots, d)
    y = jnp.zeros((n_tok + 1, d), jnp.float32).at[slot_tok].add(yb.astype(jnp.float32) * slot_gate[:, None])
    return y[:n_tok].astype(x.dtype)


def finish_layer(x, od, osb, w_o, g_ffn, w_rg, b_rg, w_re, b_re, w_g, w_u, w_d):
    b, t, d = x.shape
    mixed = jnp.concatenate([od.reshape(b, t, V_DIFF), osb.reshape(b, t, W_SB)], axis=-1).astype(x.dtype)
    h = x + mixed @ w_o
    hn = rms_norm(h, g_ffn).reshape(b * t, d)
    ids, gates = route(hn, w_rg, b_rg, w_re, b_re)
    return h + moe_experts(hn, ids, gates, w_g, w_u, w_d).reshape(b, t, d)


def setup_inputs(seed: int = 0) -> dict:
    key = jax.random.key(seed)
    ks = jax.random.split(key, 32)
    f32 = jnp.float32
    n_pages = PAST_LEN // PAGE_SIZE
    n_used = DEC_BATCH * n_pages
    n_phys = n_used + max(1, n_used // 4)

    def nrm(k, shape, scale=1.0):
        return jax.random.normal(k, shape, f32) * scale

    def gain(k, shape):
        return 1.0 + 0.02 * jax.random.normal(k, shape, f32)

    page_table = jax.random.permutation(ks[6], n_phys)[:n_used].reshape(DEC_BATCH, n_pages).astype(jnp.int32)
    return {
        'x_prompt': nrm(ks[0], (BATCH, SEQ, D_MODEL)),
        'x_sample': nrm(ks[1], (DEC_BATCH, DEC_SEQ, D_MODEL)),
        'cache_k_diff': nrm(ks[2], (DEPTH, n_phys, PAGE_SIZE, 2 * H_DIFF, DH_DIFF)),
        'cache_v_diff': nrm(ks[3], (DEPTH, n_phys, PAGE_SIZE, H_DIFF, DV_DIFF)),
        'cache_k_sb': nrm(ks[4], (DEPTH, n_phys, PAGE_SIZE, H_SB, DH_SB)),
        'cache_v_sb': nrm(ks[5], (DEPTH, n_phys, PAGE_SIZE, H_SB, DH_SB)),
        'page_table': page_table,
        'g_attn_norm': gain(ks[7], (DEPTH, D_MODEL)),
        'w_in': nrm(ks[8], (DEPTH, D_MODEL, IN_COLS), D_MODEL ** -0.5),
        'g_q_norm': gain(ks[9], (DEPTH, DH_DIFF)),
        'g_k_norm': gain(ks[10], (DEPTH, DH_DIFF)),
        'lambda_q1': nrm(ks[11], (DEPTH, DH_DIFF), 0.1),
        'lambda_k1': nrm(ks[12], (DEPTH, DH_DIFF), 0.1),
        'lambda_q2': nrm(ks[13], (DEPTH, DH_DIFF), 0.1),
        'lambda_k2': nrm(ks[14], (DEPTH, DH_DIFF), 0.1),
        'g_subln': gain(ks[15], (DEPTH, DV_DIFF)),
        'g_sb_out': gain(ks[16], (DEPTH, DH_SB)),
        'w_o': nrm(ks[17], (DEPTH, MIX_WIDTH, D_MODEL), MIX_WIDTH ** -0.5),
        'g_ffn_norm': gain(ks[18], (DEPTH, D_MODEL)),
        'w_router_group': nrm(ks[19], (DEPTH, D_MODEL, N_GROUPS), D_MODEL ** -0.5),
        'b_router_group': nrm(ks[20], (DEPTH, N_GROUPS), 0.01),
        'w_router_expert': nrm(ks[21], (DEPTH, D_MODEL, N_EXPERTS), D_MODEL ** -0.5),
        'b_router_expert': nrm(ks[22], (DEPTH, N_EXPERTS), 0.01),
        'w_exp_gate': nrm(ks[23], (DEPTH, N_EXPERTS, D_MODEL, EXPERT_FF), D_MODEL ** -0.5),
        'w_exp_up': nrm(ks[24], (DEPTH, N_EXPERTS, D_MODEL, EXPERT_FF), D_MODEL ** -0.5),
        'w_exp_down': nrm(ks[25], (DEPTH, N_EXPERTS, EXPERT_FF, D_MODEL), EXPERT_FF ** -0.5),
    }


def reference(x_prompt, x_sample, cache_k_diff, cache_v_diff, cache_k_sb, cache_v_sb, page_table,
              g_attn_norm, w_in, g_q_norm, g_k_norm, lambda_q1, lambda_k1, lambda_q2, lambda_k2,
              g_subln, g_sb_out, w_o, g_ffn_norm, w_router_group, b_router_group,
              w_router_expert, b_router_expert, w_exp_gate, w_exp_up, w_exp_down):
    f32 = jnp.float32
    past_len = page_table.shape[1] * cache_k_diff.shape[2]
    pos_p = jnp.arange(x_prompt.shape[1])
    pos_s = past_len + jnp.arange(x_sample.shape[1])
    hp, hs = x_prompt, x_sample
    kdp, vdp, ksp, vsp, kds, vds, kss, vss = [], [], [], [], [], [], [], []
    for layer in range(DEPTH):
        lam_init = 0.8 - 0.6 * math.exp(-0.3 * layer)
        lam = (jnp.exp(jnp.sum(lambda_q1[layer].astype(f32) * lambda_k1[layer].astype(f32)))
               - jnp.exp(jnp.sum(lambda_q2[layer].astype(f32) * lambda_k2[layer].astype(f32))) + lam_init)
        qd_p, kd_p, vd_p, qs_p, ks_p, vs_p = attn_inputs(hp, pos_p, g_attn_norm[layer], w_in[layer],
                                                         g_q_norm[layer], g_k_norm[layer])
        qd_s, kd_s, vd_s, qs_s, ks_s, vs_s = attn_inputs(hs, pos_s, g_attn_norm[layer], w_in[layer],
                                                         g_q_norm[layer], g_k_norm[layer])
        od_p, osb_p = prompt_attention(qd_p, kd_p, vd_p, qs_p, ks_p, vs_p, lam, g_subln[layer],
                                       lam_init, g_sb_out[layer])
        od_s, osb_s = sample_attention(qd_s, kd_s, vd_s, qs_s, ks_s, vs_s, cache_k_diff, cache_v_diff,
                                       cache_k_sb, cache_v_sb, page_table, layer, past_len, lam,
                                       g_subln[layer], lam_init, g_sb_out[layer])
        ffn = (w_o[layer], g_ffn_norm[layer], w_router_group[layer], b_router_group[layer],
               w_router_expert[layer], b_router_expert[layer], w_exp_gate[layer], w_exp_up[layer],
               w_exp_down[layer])
        hp = finish_layer(hp, od_p, osb_p, *ffn)
        hs = finish_layer(hs, od_s, osb_s, *ffn)
        kdp.append(kd_p); vdp.append(vd_p); ksp.append(ks_p); vsp.append(vs_p)
        kds.append(kd_s); vds.append(vd_s); kss.append(ks_s); vss.append(vs_s)
    return (hp, hs, jnp.stack(kdp), jnp.stack(vdp), jnp.stack(ksp), jnp.stack(vsp),
            jnp.stack(kds), jnp.stack(vds), jnp.stack(kss), jnp.stack(vss))
```

```python
import functools
import math

import jax
import jax.numpy as jnp
from jax import lax
from jax.experimental import pallas as pl
from jax.experimental.pallas import tpu as pltpu

F32 = jnp.float32
BF16 = jnp.bfloat16

NORM_EPS = 1e-6
ROPE_THETA = 10000.0
DH = 64
DV_DIFF_W = 2 * DH
LANES = 128
N_GROUPS = 4
EXPERTS_PER_GROUP = 8
NEG_BIG = -1e30
SB_DONE = -110.0
VMEM_LIMIT = 56 * 1024 * 1024


def _cparams(sem):
    return pltpu.CompilerParams(dimension_semantics=sem, vmem_limit_bytes=VMEM_LIMIT)


def _dot_t(a, b):
    return lax.dot_general(a, b, (((1,), (1,)), ((), ())), preferred_element_type=F32)


def _dot(a, b):
    return jnp.dot(a, b, preferred_element_type=F32)


def _split_bf16(x):
    hi = x.astype(BF16)
    lo = (x - hi.astype(F32)).astype(BF16)
    return hi, lo


def _proj_kernel(x_ref, g_ref, w_ref, gq_ref, gk_ref, cos_ref, sin_ref, seg_ref,
                 kd_ref, vd_ref, ks_ref, vs_ref,
                 qd_bf, kd_bf, vd_bf, qs_bf, ks_bf, vs_bf):
    x = x_ref[...]
    ms = jnp.mean(x * x, axis=-1, keepdims=True)
    xn = (x * lax.rsqrt(ms + NORM_EPS) * g_ref[...]).astype(BF16)
    width = kd_ref.shape[-1]

    def seg(c):
        return _dot(xn, w_ref[:, c * width:(c + 1) * width])

    reps = width // LANES
    cos = jnp.concatenate([cos_ref[...]] * reps, axis=1)
    sin = jnp.concatenate([sin_ref[...]] * reps, axis=1)
    lane = lax.broadcasted_iota(jnp.int32, (x.shape[0], width), 1)
    first_half = (lane % DH) < (DH // 2)

    def norm_rope(p, g):
        ms_h = _dot((p * p).astype(BF16), seg_ref[...])
        y = p * lax.rsqrt(ms_h + NORM_EPS) * g
        partner = jnp.where(first_half, pltpu.roll(y, width - DH // 2, 1), pltpu.roll(y, DH // 2, 1))
        return y * cos + partner * sin

    scale = DH ** -0.5
    qd = norm_rope(seg(0), gq_ref[...])
    qd_bf[...] = (qd * scale).astype(BF16)
    kd = norm_rope(seg(1), gk_ref[...])
    kd_ref[...] = kd
    kd_bf[...] = kd.astype(BF16)
    vd = seg(2)
    vd_ref[...] = vd
    vd_bf[...] = vd.astype(BF16)
    qs_bf[...] = (seg(3) * scale).astype(BF16)
    ks = seg(4)
    ks_ref[...] = ks
    ks_bf[...] = ks.astype(BF16)
    vs = seg(5)
    vs_ref[...] = vs
    vs_bf[...] = vs.astype(BF16)


def _rope_tables(pos, reps):
    half = DH // 2
    inv = jnp.power(jnp.float32(ROPE_THETA), -(jnp.arange(half, dtype=F32) * 2.0 / DH))
    ang = pos.astype(F32)[:, None] * inv[None, :]
    cos, sin = jnp.cos(ang), jnp.sin(ang)
    cos_t = jnp.concatenate([cos, cos] * (LANES // DH), axis=1)
    sin_t = jnp.concatenate([-sin, sin] * (LANES // DH), axis=1)
    return jnp.tile(cos_t, (reps, 1)), jnp.tile(sin_t, (reps, 1))


def _project(x2d, pos_tables, g_attn, w_in_bf, gq, gk, seg_mat, tm):
    n, d = x2d.shape
    width = w_in_bf.shape[1] // 6
    cos_t, sin_t = pos_tables
    n_pos_blocks = cos_t.shape[0] // tm
    row = lambda i: (i, 0)
    fixed = lambda i: (0, 0)
    out_f32 = jax.ShapeDtypeStruct((n, width), F32)
    out_bf = jax.ShapeDtypeStruct((n, width), BF16)
    return pl.pallas_call(
        _proj_kernel,
        grid=(n // tm,),
        in_specs=[
            pl.BlockSpec((tm, d), row),
            pl.BlockSpec((1, d), fixed),
            pl.BlockSpec(w_in_bf.shape, fixed),
            pl.BlockSpec((1, width), fixed),
            pl.BlockSpec((1, width), fixed),
            pl.BlockSpec((tm, LANES), lambda i: (i % n_pos_blocks, 0)),
            pl.BlockSpec((tm, LANES), lambda i: (i % n_pos_blocks, 0)),
            pl.BlockSpec(seg_mat.shape, fixed),
        ],
        out_specs=[pl.BlockSpec((tm, width), row)] * 10,
        out_shape=[out_f32] * 4 + [out_bf] * 6,
        compiler_params=_cparams(("arbitrary",)),
        name="in_proj",
    )(x2d, g_attn, w_in_bf, gq, gk, cos_t, sin_t, seg_mat)


def _half_masks(shape):
    lane = lax.broadcasted_iota(jnp.int32, shape, 1)
    return lane < DH


def _diff_prompt_kernel(lam_ref, q_ref, k_ref, v_ref, g_ref, o_ref, *, tq, tk, out_scale):
    i = pl.program_id(1)
    lam = lam_ref[0, 0]
    n_heads = q_ref.shape[-1] // LANES
    lo_half = _half_masks((tq, LANES))
    row = lax.broadcasted_iota(jnp.int32, (tq, tk), 0)
    col = lax.broadcasted_iota(jnp.int32, (tq, tk), 1)
    zero = jnp.zeros((), BF16)

    for h in range(n_heads):
        ls = slice(h * LANES, (h + 1) * LANES)
        qh = q_ref[0, :, ls]
        q_maps = (jnp.where(lo_half, qh, zero), jnp.where(lo_half, zero, qh))

        def step(j, carry, masked, q_maps=q_maps, ls=ls):
            start = pl.multiple_of(j * tk, tk)
            kb = k_ref[0, pl.ds(start, tk), ls]
            vb = v_ref[0, pl.ds(start, tk), ls]
            new = []
            for mp in range(2):
                m, l, acc = carry[3 * mp:3 * mp + 3]
                s = _dot_t(q_maps[mp], kb)
                if masked:
                    s = jnp.where(j * tk + col <= i * tq + row, s, NEG_BIG)
                m_new = jnp.maximum(m, jnp.max(s, axis=-1, keepdims=True))
                alpha = jnp.exp(m - m_new)
                p = jnp.exp(s - m_new)
                l = alpha * l + jnp.sum(p, axis=-1, keepdims=True)
                acc = alpha * acc + _dot(p.astype(BF16), vb)
                new += [m_new, l, acc]
            return tuple(new)

        init = (jnp.full((tq, 1), NEG_BIG, F32), jnp.zeros((tq, 1), F32), jnp.zeros((tq, LANES), F32)) * 2
        n_full = i * (tq // tk)
        carry = lax.fori_loop(0, n_full, functools.partial(step, masked=False), init)
        for d in range(tq // tk):
            carry = step(n_full + d, carry, True)
        _, l0, a0, _, l1, a1 = carry
        o = a0 / l0 - lam * (a1 / l1)
        ms = jnp.mean(o * o, axis=-1, keepdims=True)
        o_ref[0, :, ls] = (o * lax.rsqrt(ms + NORM_EPS) * g_ref[...] * out_scale).astype(o_ref.dtype)


def _diff_prompt(lam, q, k, v, g_subln, out_scale, tq, tk):
    b, s, w = q.shape
    kern = functools.partial(_diff_prompt_kernel, tq=tq, tk=tk, out_scale=out_scale)
    return pl.pallas_call(
        kern,
        grid=(b, s // tq),
        in_specs=[
            pl.BlockSpec(memory_space=pltpu.SMEM),
            pl.BlockSpec((1, tq, w), lambda bi, i: (bi, i, 0)),
            pl.BlockSpec((1, s, w), lambda bi, i: (bi, 0, 0)),
            pl.BlockSpec((1, s, w), lambda bi, i: (bi, 0, 0)),
            pl.BlockSpec((1, LANES), lambda bi, i: (0, 0)),
        ],
        out_specs=pl.BlockSpec((1, tq, w), lambda bi, i: (bi, i, 0)),
        out_shape=jax.ShapeDtypeStruct((b, s, w), BF16),
        compiler_params=_cparams(("arbitrary", "arbitrary")),
        name="diff_prompt",
    )(lam, q, k, v, g_subln)


def _sb_block(z, valid, tri_ref, r_in):
    log_1m = -(jnp.maximum(z, 0.0) + jnp.log1p(jnp.exp(-jnp.abs(z))))
    if valid is not None:
        log_1m = jnp.where(valid, log_1m, 0.0)
    hi, lo = _split_bf16(log_1m)
    tri = tri_ref[...]
    later = _dot(hi, tri) + _dot(lo, tri)
    a = jnp.exp(z + log_1m + later + r_in)
    if valid is not None:
        a = jnp.where(valid, a, 0.0)
    r_out = r_in + later[:, 0:1] + log_1m[:, 0:1]
    return a, r_out


def _head_rms(o, seg_ref, g):
    ms = _dot((o * o).astype(BF16), seg_ref[...])
    return o * lax.rsqrt(ms + NORM_EPS) * g


def _sb_prompt_kernel(q_ref, k_ref, v_ref, g_ref, tri_ref, seg_ref, o_ref, *, tq, tk):
    i = pl.program_id(1)
    n_blocks = q_ref.shape[-1] // LANES
    lo_half = _half_masks((tq, LANES))
    row = lax.broadcasted_iota(jnp.int32, (tq, tk), 0)
    col = lax.broadcasted_iota(jnp.int32, (tq, tk), 1)
    zero = jnp.zeros((), BF16)

    for p in range(n_blocks):
        ls = slice(p * LANES, (p + 1) * LANES)
        qh = q_ref[0, :, ls]
        q_heads = (jnp.where(lo_half, qh, zero), jnp.where(lo_half, zero, qh))

        def step(j, state, masked, q_heads=q_heads, ls=ls):
            start = pl.multiple_of(j * tk, tk)
            kb = k_ref[0, pl.ds(start, tk), ls]
            vb = v_ref[0, pl.ds(start, tk), ls]
            valid = (j * tk + col < i * tq + row) if masked else None
            new = []
            for hd in range(2):
                r, acc = state[2 * hd:2 * hd + 2]
                a, r = _sb_block(_dot_t(q_heads[hd], kb), valid, tri_ref, r)
                new += [r, acc + _dot(a.astype(BF16), vb)]
            return tuple(new)

        state = (jnp.zeros((tq, 1), F32), jnp.zeros((tq, LANES), F32)) * 2
        n_full = i * (tq // tk)
        for d in reversed(range(tq // tk)):
            state = step(n_full + d, state, True)

        def unfinished(st):
            return (jnp.max(jnp.maximum(st[0], st[2])) >= SB_DONE).astype(jnp.int32)

        def not_done(c):
            return jnp.logical_and(c[0] >= 0, c[1] > 0)

        def body(c):
            st = step(c[0], c[2:], False)
            return (c[0] - 1, unfinished(st)) + st

        _, _, _, a0, _, a1 = lax.while_loop(not_done, body, (n_full - 1, unfinished(state)) + state)
        o = jnp.where(lo_half, a0, a1)
        o_ref[0, :, ls] = _head_rms(o, seg_ref, g_ref[...]).astype(o_ref.dtype)


def _sb_prompt(q, k, v, g_sb, tri, seg, tq, tk):
    b, s, w = q.shape
    kern = functools.partial(_sb_prompt_kernel, tq=tq, tk=tk)
    return pl.pallas_call(
        kern,
        grid=(b, s // tq),
        in_specs=[
            pl.BlockSpec((1, tq, w), lambda bi, i: (bi, i, 0)),
            pl.BlockSpec((1, s, w), lambda bi, i: (bi, 0, 0)),
            pl.BlockSpec((1, s, w), lambda bi, i: (bi, 0, 0)),
            pl.BlockSpec((1, LANES), lambda bi, i: (0, 0)),
            pl.BlockSpec(tri.shape, lambda bi, i: (0, 0)),
            pl.BlockSpec(seg.shape, lambda bi, i: (0, 0)),
        ],
        out_specs=pl.BlockSpec((1, tq, w), lambda bi, i: (bi, i, 0)),
        out_shape=jax.ShapeDtypeStruct((b, s, w), BF16),
        compiler_params=_cparams(("arbitrary", "arbitrary")),
        name="sb_prompt",
    )(q, k, v, g_sb, tri, seg)


Q_ROWS = 16


def _page_spec(n_pages, first_of_chunk, r, width, page_size):
    def index_map(b, c, pt_ref):
        return (pt_ref[b * n_pages + first_of_chunk(c) + r], 0, 0)
    return pl.BlockSpec((1, page_size, width), index_map)


def _gather_pages(refs, ls):
    return jnp.concatenate([r[0, :, ls] for r in refs], axis=0).astype(BF16)


def _diff_sample_kernel(pt_ref, lam_ref, q_ref, kn_ref, vn_ref, g_ref, *rest, n_pp, t_new, out_scale):
    k_refs, v_refs = rest[:n_pp], rest[n_pp:2 * n_pp]
    o_ref, m_sc, l_sc, acc_sc = rest[2 * n_pp:]
    c = pl.program_id(1)
    n_heads = q_ref.shape[-1] // LANES

    @pl.when(c == 0)
    def _():
        m_sc[...] = jnp.full(m_sc.shape, NEG_BIG, F32)
        l_sc[...] = jnp.zeros(l_sc.shape, F32)
        acc_sc[...] = jnp.zeros(acc_sc.shape, F32)

    def update(h, s, vb):
        m = m_sc[h]
        m_new = jnp.maximum(m, jnp.max(s, axis=-1, keepdims=True))
        alpha = jnp.exp(m - m_new)
        p = jnp.exp(s - m_new)
        l_sc[h] = alpha * l_sc[h] + jnp.sum(p, axis=-1, keepdims=True)
        acc_sc[h] = alpha * acc_sc[h] + _dot(p.astype(BF16), vb)
        m_sc[h] = m_new

    for h in range(n_heads):
        ls = slice(h * LANES, (h + 1) * LANES)
        update(h, _dot_t(q_ref[0, :, ls], _gather_pages(k_refs, ls)), _gather_pages(v_refs, ls))

    @pl.when(c == pl.num_programs(1) - 1)
    def _():
        lam = lam_ref[0, 0]
        row = lax.broadcasted_iota(jnp.int32, (Q_ROWS, Q_ROWS), 0)
        col = lax.broadcasted_iota(jnp.int32, (Q_ROWS, Q_ROWS), 1)
        valid = jnp.logical_and(col < t_new, col <= row % t_new)
        for h in range(n_heads):
            ls = slice(h * LANES, (h + 1) * LANES)
            s = jnp.where(valid, _dot_t(q_ref[0, :, ls], kn_ref[0, :, ls]), NEG_BIG)
            update(h, s, vn_ref[0, :, ls])
            o_maps = acc_sc[h] / l_sc[h]
            o = o_maps[0:8] - lam * o_maps[t_new:t_new + 8]
            ms = jnp.mean(o * o, axis=-1, keepdims=True)
            o_ref[0, :, ls] = o * lax.rsqrt(ms + NORM_EPS) * g_ref[...] * out_scale


def _diff_sample(pt_flat, lam, q16, k_new, v_new, g_subln, cache_k, cache_v, n_pages, n_pp, t_new, out_scale):
    nb, _, w = q16.shape
    page_size = cache_k.shape[1]
    n_heads = w // LANES
    per_seq = lambda b, c, pt: (b, 0, 0)
    chunk0 = lambda c: c * n_pp
    kern = functools.partial(_diff_sample_kernel, n_pp=n_pp, t_new=t_new, out_scale=out_scale)
    grid_spec = pltpu.PrefetchScalarGridSpec(
        num_scalar_prefetch=1,
        grid=(nb, n_pages // n_pp),
        in_specs=[
            pl.BlockSpec(memory_space=pltpu.SMEM),
            pl.BlockSpec((1, Q_ROWS, w), per_seq),
            pl.BlockSpec((1, Q_ROWS, w), per_seq),
            pl.BlockSpec((1, Q_ROWS, w), per_seq),
            pl.BlockSpec((1, LANES), lambda b, c, pt: (0, 0)),
        ] + [_page_spec(n_pages, chunk0, r, w, page_size) for r in range(n_pp)] * 2,
        out_specs=pl.BlockSpec((1, 8, w), per_seq),
        scratch_shapes=[
            pltpu.VMEM((n_heads, Q_ROWS, 1), F32),
            pltpu.VMEM((n_heads, Q_ROWS, 1), F32),
            pltpu.VMEM((n_heads, Q_ROWS, LANES), F32),
        ],
    )
    return pl.pallas_call(
        kern,
        grid_spec=grid_spec,
        out_shape=jax.ShapeDtypeStruct((nb, 8, w), F32),
        compiler_params=_cparams(("arbitrary", "arbitrary")),
        name="diff_sample",
    )(pt_flat, lam, q16, k_new, v_new, g_subln, *([cache_k] * n_pp), *([cache_v] * n_pp))


def _sb_sample_kernel(pt_ref, q_ref, kn_ref, vn_ref, g_ref, tri_ref, trin_ref, seg_ref, acc_in, r_in, *rest,
                      n_pp, t_new, with_new):
    k_refs, v_refs = rest[:n_pp], rest[n_pp:2 * n_pp]
    o_ref, acc_ref, r_ref = rest[2 * n_pp:]
    c = pl.program_id(1)
    n_blocks = q_ref.shape[-1] // LANES

    @pl.when(c == 0)
    def _():
        if with_new:
            row = lax.broadcasted_iota(jnp.int32, (Q_ROWS, Q_ROWS), 0)
            col = lax.broadcasted_iota(jnp.int32, (Q_ROWS, Q_ROWS), 1)
            valid = jnp.logical_and(col < t_new, col < row % t_new)
            for p in range(n_blocks):
                ls = slice(p * LANES, (p + 1) * LANES)
                z = _dot_t(q_ref[0, :, ls], kn_ref[0, :, ls])
                a, r = _sb_block(z, valid, trin_ref, jnp.zeros((Q_ROWS, 1), F32))
                acc_ref[0, :, ls] = _dot(a.astype(BF16), vn_ref[0, :, ls])
                r_ref[0, :, ls] = jnp.broadcast_to(r, (Q_ROWS, LANES))
        else:
            acc_ref[...] = acc_in[...]
            r_ref[...] = r_in[...]

    for p in range(n_blocks):
        ls = slice(p * LANES, (p + 1) * LANES)
        z = _dot_t(q_ref[0, :, ls], _gather_pages(k_refs, ls))
        a, r = _sb_block(z, None, tri_ref, r_ref[0, :, p * LANES:p * LANES + 1])
        acc_ref[0, :, ls] += _dot(a.astype(BF16), _gather_pages(v_refs, ls))
        r_ref[0, :, ls] = jnp.broadcast_to(r, (Q_ROWS, LANES))

    @pl.when(c == pl.num_programs(1) - 1)
    def _():
        lo_half = _half_masks((8, LANES))
        for p in range(n_blocks):
            ls = slice(p * LANES, (p + 1) * LANES)
            acc = acc_ref[0, :, ls]
            o = jnp.where(lo_half, acc[0:8], acc[t_new:t_new + 8])
            o_ref[0, :, ls] = _head_rms(o, seg_ref, g_ref[...])


def _sb_sample(pt_flat, q16, k_new, v_new, g_sb, tri, tri_new, seg, acc_in, r_in, cache_k, cache_v,
               n_pages, page_hi, n_chunks, n_pp, t_new, with_new):
    nb, _, w = q16.shape
    page_size = cache_k.shape[1]
    per_seq = lambda b, c, pt: (b, 0, 0)
    fixed = lambda b, c, pt: (0, 0)
    first = lambda c: page_hi - (c + 1) * n_pp
    kern = functools.partial(_sb_sample_kernel, n_pp=n_pp, t_new=t_new, with_new=with_new)
    state_spec = pl.BlockSpec((1, Q_ROWS, w), per_seq)
    grid_spec = pltpu.PrefetchScalarGridSpec(
        num_scalar_prefetch=1,
        grid=(nb, n_chunks),
        in_specs=[
            state_spec, state_spec, state_spec,
            pl.BlockSpec((1, LANES), fixed),
            pl.BlockSpec(tri.shape, fixed),
            pl.BlockSpec(tri_new.shape, fixed),
            pl.BlockSpec(seg.shape, fixed),
            state_spec, state_spec,
        ] + [_page_spec(n_pages, first, r, w, page_size) for r in range(n_pp)] * 2,
        out_specs=[pl.BlockSpec((1, 8, w), per_seq), state_spec, state_spec],
    )
    state = jax.ShapeDtypeStruct((nb, Q_ROWS, w), F32)
    return pl.pallas_call(
        kern,
        grid_spec=grid_spec,
        out_shape=[jax.ShapeDtypeStruct((nb, 8, w), F32), state, state],
        compiler_params=_cparams(("arbitrary", "arbitrary")),
        name="sb_sample_new" if with_new else "sb_sample_rest",
    )(pt_flat, q16, k_new, v_new, g_sb, tri, tri_new, seg, acc_in, r_in,
      *([cache_k] * n_pp), *([cache_v] * n_pp))


def _finish_kernel(x_ref, od_ref, os_ref, wo_ref, g_ref, wr_hi_ref, wr_lo_ref, br_ref,
                   h_ref, hn_ref, route_ref):
    half = od_ref.shape[-1]
    h = x_ref[...] + _dot(od_ref[...], wo_ref[0:half, :]) + _dot(os_ref[...], wo_ref[half:, :])
    h_ref[...] = h
    ms = jnp.mean(h * h, axis=-1, keepdims=True)
    hn = h * lax.rsqrt(ms + NORM_EPS) * g_ref[...]
    hn_ref[...] = hn.astype(BF16)

    hn_hi, hn_lo = _split_bf16(hn)
    logits = (_dot(hn_hi, wr_hi_ref[...]) + _dot(hn_lo, wr_hi_ref[...]) + _dot(hn_hi, wr_lo_ref[...])
              + br_ref[...])
    n_exp = N_GROUPS * EXPERTS_PER_GROUP
    lane = lax.broadcasted_iota(jnp.int32, logits.shape, 1)

    def first_argmax(v, vmax):
        return jnp.min(jnp.where(v == vmax, lane, LANES), axis=-1, keepdims=True)

    lg = jnp.where(jnp.logical_and(lane >= n_exp, lane < n_exp + N_GROUPS), logits, NEG_BIG)
    mg = jnp.max(lg, axis=-1, keepdims=True)
    grp = first_argmax(lg, mg) - n_exp
    p_top = 1.0 / jnp.sum(jnp.exp(lg - mg), axis=-1, keepdims=True)

    le = jnp.where(jnp.logical_and(lane < n_exp, lane // EXPERTS_PER_GROUP == grp), logits, NEG_BIG)
    m1 = jnp.max(le, axis=-1, keepdims=True)
    i1 = first_argmax(le, m1)
    le2 = jnp.where(lane == i1, NEG_BIG, le)
    m2 = jnp.max(le2, axis=-1, keepdims=True)
    i2 = first_argmax(le2, m2)
    e2 = jnp.exp(m2 - m1)
    gate1 = p_top / (1.0 + e2)
    gate2 = p_top * e2 / (1.0 + e2)
    route = jnp.where(lane == 0, i1.astype(F32),
                      jnp.where(lane == 1, i2.astype(F32),
                                jnp.where(lane == 2, gate1, jnp.where(lane == 3, gate2, 0.0))))
    route_ref[...] = route


def _finish(x2d, od, osb, w_o_bf, g_ffn, wr_hi, wr_lo, b_r, tm):
    n, d = x2d.shape
    half = od.shape[-1]
    row = lambda i: (i, 0)
    fixed = lambda i: (0, 0)
    return pl.pallas_call(
        _finish_kernel,
        grid=(n // tm,),
        in_specs=[
            pl.BlockSpec((tm, d), row),
            pl.BlockSpec((tm, half), row),
            pl.BlockSpec((tm, half), row),
            pl.BlockSpec(w_o_bf.shape, fixed),
            pl.BlockSpec((1, d), fixed),
            pl.BlockSpec(wr_hi.shape, fixed),
            pl.BlockSpec(wr_lo.shape, fixed),
            pl.BlockSpec((1, LANES), fixed),
        ],
        out_specs=[pl.BlockSpec((tm, d), row), pl.BlockSpec((tm, d), row), pl.BlockSpec((tm, LANES), row)],
        out_shape=[jax.ShapeDtypeStruct((n, d), F32), jax.ShapeDtypeStruct((n, d), BF16),
                   jax.ShapeDtypeStruct((n, LANES), F32)],
        compiler_params=_cparams(("arbitrary",)),
        name="out_proj_router",
    )(x2d, od, osb, w_o_bf, g_ffn, wr_hi, wr_lo, b_r)


def _expert_kernel(be_ref, nact_ref, x_ref, wg_ref, wu_ref, wd_ref, y_ref):
    blk = pl.program_id(0)

    @pl.when(blk < nact_ref[0])
    def _():
        x = x_ref[...]
        gate = _dot(x, wg_ref[0])
        up = _dot(x, wu_ref[0])
        hidden = (gate * jax.nn.sigmoid(gate) * up).astype(BF16)
        y_ref[...] = _dot(hidden, wd_ref[0])

    @pl.when(blk >= nact_ref[0])
    def _():
        y_ref[...] = jnp.zeros(y_ref.shape, y_ref.dtype)


def _expert_ffn(block_exp, n_active, xs, w_gate, w_up, w_down, bm):
    n_slots, d = xs.shape
    ff = w_gate.shape[-1]
    grid_spec = pltpu.PrefetchScalarGridSpec(
        num_scalar_prefetch=2,
        grid=(n_slots // bm,),
        in_specs=[
            pl.BlockSpec((bm, d), lambda i, be, na: (i, 0)),
            pl.BlockSpec((1, d, ff), lambda i, be, na: (be[i], 0, 0)),
            pl.BlockSpec((1, d, ff), lambda i, be, na: (be[i], 0, 0)),
            pl.BlockSpec((1, ff, d), lambda i, be, na: (be[i], 0, 0)),
        ],
        out_specs=pl.BlockSpec((bm, d), lambda i, be, na: (i, 0)),
    )
    return pl.pallas_call(
        _expert_kernel,
        grid_spec=grid_spec,
        out_shape=jax.ShapeDtypeStruct((n_slots, d), F32),
        compiler_params=_cparams(("arbitrary",)),
        name="expert_ffn",
    )(block_exp, n_active, xs, w_gate, w_up, w_down)


def _moe(h, hn_bf, route, w_gate, w_up, w_down, bm):
    n_tok, d = h.shape
    n_exp = w_gate.shape[0]
    ids = route[:, 0:2].astype(jnp.int32)
    gates = route[:, 2:4]
    n_assign = 2 * n_tok
    flat_e = ids.reshape(-1)
    order = jnp.argsort(flat_e, stable=True)
    e_sorted = flat_e[order]
    counts = jnp.zeros((n_exp,), jnp.int32).at[flat_e].add(1)
    padded = (counts + bm - 1) // bm * bm
    start = jnp.cumsum(counts) - counts
    pend = jnp.cumsum(padded)
    pstart = pend - padded
    dest_sorted = pstart[e_sorted] + (jnp.arange(n_assign, dtype=jnp.int32) - start[e_sorted])
    n_blocks = -(-n_assign // bm) + n_exp
    n_slots = n_blocks * bm
    slot_tok = jnp.zeros((n_slots,), jnp.int32).at[dest_sorted].set(order // 2)
    dest = jnp.zeros((n_assign,), jnp.int32).at[order].set(dest_sorted).reshape(n_tok, 2)
    block_exp = jnp.minimum(
        jnp.searchsorted(pend, jnp.arange(n_blocks, dtype=jnp.int32) * bm, side='right'), n_exp - 1
    ).astype(jnp.int32)
    n_active = (pend[-1:] // bm).astype(jnp.int32)
    xs = hn_bf[slot_tok]
    yb = _expert_ffn(block_exp, n_active, xs, w_gate, w_up, w_down, bm)
    return h + (gates[:, 0:1] * yb[dest[:, 0]] + gates[:, 1:2] * yb[dest[:, 1]])


def _block_diag_mean(width, group):
    idx = jnp.arange(width) // group
    return jnp.where(idx[:, None] == idx[None, :], 1.0 / group, 0.0).astype(BF16)


def _tri(n):
    idx = jnp.arange(n)
    return (idx[:, None] > idx[None, :]).astype(BF16)


def _sample_rows(a, nb, t):
    w = a.shape[-1]
    a = a.reshape(nb, t, w)
    lo = (jnp.arange(w) % LANES) < DH
    zeros = jnp.zeros((nb, Q_ROWS - 2 * t, w), a.dtype)
    return jnp.concatenate([jnp.where(lo, a, 0), jnp.where(lo, 0, a), zeros], axis=1)


def _pad_rows(a, nb, t):
    w = a.shape[-1]
    a = a.reshape(nb, t, w)
    return jnp.concatenate([a, jnp.zeros((nb, Q_ROWS - t, w), a.dtype)], axis=1)


def kernel(x_prompt, x_sample, cache_k_diff, cache_v_diff, cache_k_sb, cache_v_sb, page_table, g_attn_norm, w_in, g_q_norm, g_k_norm, lambda_q1, lambda_k1, lambda_q2, lambda_k2, g_subln, g_sb_out, w_o, g_ffn_norm, w_router_group, b_router_group, w_router_expert, b_router_expert, w_exp_gate, w_exp_up, w_exp_down):
    depth = w_in.shape[0]
    b, s, d = x_prompt.shape
    nb, t_new, _ = x_sample.shape
    n_phys, page_size = cache_k_diff.shape[1], cache_k_diff.shape[2]
    n_pages = page_table.shape[1]
    past_len = n_pages * page_size
    width = w_in.shape[-1] // 6
    n_exp = w_exp_gate.shape[1]

    tm = min(256, nb * t_new)
    tq = tk = min(256, s)
    n_pp_diff = min(8, n_pages)
    n_pp_sb = min(4, n_pages)
    bm = 256

    seg64 = _block_diag_mean(width, DH)
    seg64_lane = _block_diag_mean(LANES, DH)
    tri_p = _tri(tk)
    tri_s = _tri(n_pp_sb * page_size)
    tri_new = _tri(Q_ROWS)
    pos_p = _rope_tables(jnp.arange(s), 1)
    pos_s = _rope_tables(past_len + jnp.arange(t_new), tm // t_new)

    hp = x_prompt.reshape(b * s, d)
    hs = x_sample.reshape(nb * t_new, d)
    outs = [[] for _ in range(8)]
    for layer in range(depth):
        lam_init = 0.8 - 0.6 * math.exp(-0.3 * layer)
        lam = (jnp.exp(jnp.sum(lambda_q1[layer].astype(F32) * lambda_k1[layer].astype(F32)))
               - jnp.exp(jnp.sum(lambda_q2[layer].astype(F32) * lambda_k2[layer].astype(F32))) + lam_init)
        lam = lam.reshape(1, 1).astype(F32)
        out_scale = 1.0 - lam_init
        g_attn = g_attn_norm[layer].reshape(1, d)
        w_in_bf = w_in[layer].astype(BF16)
        gq = jnp.tile(g_q_norm[layer], width // DH).reshape(1, width)
        gk = jnp.tile(g_k_norm[layer], width // DH).reshape(1, width)
        g_sub = g_subln[layer].reshape(1, LANES)
        g_sb = jnp.tile(g_sb_out[layer], LANES // DH).reshape(1, LANES)

        (kd_p, vd_p, ks_p, vs_p, qd_pb, kd_pb, vd_pb, qs_pb, ks_pb, vs_pb) = _project(
            hp, pos_p, g_attn, w_in_bf, gq, gk, seg64, min(256, s))
        (kd_s, vd_s, ks_s, vs_s, qd_sb, kd_sb, vd_sb, qs_sb, ks_sb, vs_sb) = _project(
            hs, pos_s, g_attn, w_in_bf, gq, gk, seg64, tm)

        r3 = lambda a: a.reshape(b, s, width)
        od_p = _diff_prompt(lam, r3(qd_pb), r3(kd_pb), r3(vd_pb), g_sub, out_scale, tq, tk)
        osb_p = _sb_prompt(r3(qs_pb), r3(ks_pb), r3(vs_pb), g_sb, tri_p, seg64_lane, tq, tk)

        pt_flat = (page_table.astype(jnp.int32) + layer * n_phys).reshape(-1)
        flat_cache = lambda c: c.reshape(depth * n_phys, page_size, width)
        od_s = _diff_sample(pt_flat, lam, _sample_rows(qd_sb, nb, t_new), _pad_rows(kd_sb, nb, t_new),
                            _pad_rows(vd_sb, nb, t_new), g_sub, flat_cache(cache_k_diff),
                            flat_cache(cache_v_diff), n_pages, n_pp_diff, t_new, out_scale)
        q16s = _sample_rows(qs_sb, nb, t_new)
        kn16, vn16 = _pad_rows(ks_sb, nb, t_new), _pad_rows(vs_sb, nb, t_new)
        cks, cvs = flat_cache(cache_k_sb), flat_cache(cache_v_sb)
        zeros_state = jnp.zeros((nb, Q_ROWS, width), F32)
        sb_args = (pt_flat, q16s, kn16, vn16, g_sb, tri_s, tri_new, seg64_lane)
        osb_s, acc_s, r_s = _sb_sample(*sb_args, zeros_state, zeros_state, cks, cvs,
                                       n_pages, n_pages, 1, n_pp_sb, t_new, True)
        n_rest = (n_pages - n_pp_sb) // n_pp_sb
        if n_rest > 0:
            unfinished = jnp.max(r_s[:, :2 * t_new, :]) >= SB_DONE
            osb_s = lax.cond(
                unfinished,
                lambda: _sb_sample(*sb_args, acc_s, r_s, cks, cvs, n_pages, n_pages - n_pp_sb,
                                   n_rest, n_pp_sb, t_new, False)[0],
                lambda: osb_s)

        w_o_bf = w_o[layer].astype(BF16)
        g_ffn = g_ffn_norm[layer].reshape(1, d)
        w_r = jnp.zeros((d, LANES), F32)
        w_r = w_r.at[:, :n_exp].set(w_router_expert[layer]).at[:, n_exp:n_exp + N_GROUPS].set(w_router_group[layer])
        wr_hi, wr_lo = _split_bf16(w_r)
        b_r = jnp.zeros((1, LANES), F32)
        b_r = b_r.at[0, :n_exp].set(b_router_expert[layer]).at[0, n_exp:n_exp + N_GROUPS].set(b_router_group[layer])

        od_s2 = od_s[:, :t_new].reshape(nb * t_new, width).astype(BF16)
        osb_s2 = osb_s[:, :t_new].reshape(nb * t_new, width).astype(BF16)
        fin = lambda xx, od, osb, tile: _finish(xx, od, osb, w_o_bf, g_ffn, wr_hi, wr_lo, b_r, tile)
        h_p, hn_p, route_p = fin(hp, od_p.reshape(b * s, width), osb_p.reshape(b * s, width), min(256, s))
        h_s, hn_s, route_s = fin(hs, od_s2, osb_s2, tm)

        y = _moe(jnp.concatenate([h_p, h_s]), jnp.concatenate([hn_p, hn_s]), jnp.concatenate([route_p, route_s]),
                 w_exp_gate[layer].astype(BF16), w_exp_up[layer].astype(BF16), w_exp_down[layer].astype(BF16), bm)
        hp, hs = y[:b * s], y[b * s:]

        for lst, val in zip(outs, (kd_p, vd_p, ks_p, vs_p, kd_s, vd_s, ks_s, vs_s)):
            lst.append(val)

    h_diff, h_sb = width // DV_DIFF_W, width // DH
    shapes_p = [(b, s, 2 * h_diff, DH), (b, s, h_diff, DV_DIFF_W), (b, s, h_sb, DH), (b, s, h_sb, DH)]
    shapes_s = [(nb, t_new) + sh[2:] for sh in shapes_p]
    stacked = [jnp.stack([v.reshape(sh) for v in lst]) for lst, sh in zip(outs, shapes_p + shapes_s)]
    return (hp.reshape(b, s, d), hs.reshape(nb, t_new, d), *stacked)
```

```python
import functools
import math

import jax
import jax.numpy as jnp
from jax import lax
from jax.experimental import pallas as pl
from jax.experimental.pallas import tpu as pltpu

F32 = jnp.float32
BF16 = jnp.bfloat16

NORM_EPS = 1e-6
ROPE_THETA = 10000.0
DH = 64
DV_DIFF_W = 2 * DH
LANES = 128
N_GROUPS = 4
EXPERTS_PER_GROUP = 8
NEG_BIG = -1e30
SB_DONE = -110.0
VMEM_LIMIT = 56 * 1024 * 1024
BOUND_MAX = 40.0
BOUND_SLACK = 1.001
PROMPT_TQ = 256
PROMPT_TK = 256


def _cparams(sem):
    return pltpu.CompilerParams(dimension_semantics=sem, vmem_limit_bytes=VMEM_LIMIT)


def _dot_t(a, b):
    return lax.dot_general(a, b, (((1,), (1,)), ((), ())), preferred_element_type=F32)


def _dot(a, b):
    return jnp.dot(a, b, preferred_element_type=F32)


def _split_bf16(x):
    hi = x.astype(BF16)
    lo = (x - hi.astype(F32)).astype(BF16)
    return hi, lo


def _proj_kernel(x_ref, g_ref, w_ref, gq_ref, gk_ref, cos_ref, sin_ref, seg_ref,
                 kd_ref, vd_ref, ks_ref, vs_ref,
                 qd_bf, kd_bf, vd_bf, qs_bf, ks_bf, vs_bf):
    x = x_ref[...]
    ms = jnp.mean(x * x, axis=-1, keepdims=True)
    xn = (x * lax.rsqrt(ms + NORM_EPS) * g_ref[...]).astype(BF16)
    width = kd_ref.shape[-1]

    def seg(c):
        return _dot(xn, w_ref[:, c * width:(c + 1) * width])

    reps = width // LANES
    cos = jnp.concatenate([cos_ref[...]] * reps, axis=1)
    sin = jnp.concatenate([sin_ref[...]] * reps, axis=1)
    lane = lax.broadcasted_iota(jnp.int32, (x.shape[0], width), 1)
    first_half = (lane % DH) < (DH // 2)

    def norm_rope(p, g):
        ms_h = _dot((p * p).astype(BF16), seg_ref[...])
        y = p * lax.rsqrt(ms_h + NORM_EPS) * g
        partner = jnp.where(first_half, pltpu.roll(y, width - DH // 2, 1), pltpu.roll(y, DH // 2, 1))
        return y * cos + partner * sin

    scale = DH ** -0.5
    qd = norm_rope(seg(0), gq_ref[...])
    qd_bf[...] = (qd * scale).astype(BF16)
    kd = norm_rope(seg(1), gk_ref[...])
    kd_ref[...] = kd
    kd_bf[...] = kd.astype(BF16)
    vd = seg(2)
    vd_ref[...] = vd
    vd_bf[...] = vd.astype(BF16)
    qs_bf[...] = (seg(3) * scale).astype(BF16)
    ks = seg(4)
    ks_ref[...] = ks
    ks_bf[...] = ks.astype(BF16)
    vs = seg(5)
    vs_ref[...] = vs
    vs_bf[...] = vs.astype(BF16)


def _rope_tables(pos, reps):
    half = DH // 2
    inv = jnp.power(jnp.float32(ROPE_THETA), -(jnp.arange(half, dtype=F32) * 2.0 / DH))
    ang = pos.astype(F32)[:, None] * inv[None, :]
    cos, sin = jnp.cos(ang), jnp.sin(ang)
    cos_t = jnp.concatenate([cos, cos] * (LANES // DH), axis=1)
    sin_t = jnp.concatenate([-sin, sin] * (LANES // DH), axis=1)
    return jnp.tile(cos_t, (reps, 1)), jnp.tile(sin_t, (reps, 1))


def _project(x2d, pos_tables, g_attn, w_in_bf, gq, gk, seg_mat, tm):
    n, d = x2d.shape
    width = w_in_bf.shape[1] // 6
    cos_t, sin_t = pos_tables
    n_pos_blocks = cos_t.shape[0] // tm
    row = lambda i: (i, 0)
    fixed = lambda i: (0, 0)
    out_f32 = jax.ShapeDtypeStruct((n, width), F32)
    out_bf = jax.ShapeDtypeStruct((n, width), BF16)
    return pl.pallas_call(
        _proj_kernel,
        grid=(n // tm,),
        in_specs=[
            pl.BlockSpec((tm, d), row),
            pl.BlockSpec((1, d), fixed),
            pl.BlockSpec(w_in_bf.shape, fixed),
            pl.BlockSpec((1, width), fixed),
            pl.BlockSpec((1, width), fixed),
            pl.BlockSpec((tm, LANES), lambda i: (i % n_pos_blocks, 0)),
            pl.BlockSpec((tm, LANES), lambda i: (i % n_pos_blocks, 0)),
            pl.BlockSpec(seg_mat.shape, fixed),
        ],
        out_specs=[pl.BlockSpec((tm, width), row)] * 10,
        out_shape=[out_f32] * 4 + [out_bf] * 6,
        compiler_params=_cparams(("arbitrary",)),
        name="in_proj",
    )(x2d, g_attn, w_in_bf, gq, gk, cos_t, sin_t, seg_mat)


def _half_masks(shape):
    lane = lax.broadcasted_iota(jnp.int32, shape, 1)
    return lane < DH


def _stack_halves(q_ref, q_sc, tq):
    lo_half = _half_masks((tq, LANES))
    zero = jnp.zeros((), BF16)
    for h in range(q_sc.shape[0]):
        qh = q_ref[0, :, h * LANES:(h + 1) * LANES]
        q_sc[h, 0:tq, :] = jnp.where(lo_half, qh, zero)
        q_sc[h, tq:2 * tq, :] = jnp.where(lo_half, zero, qh)


def _diff_prompt_kernel(lam_ref, q_ref, k_ref, v_ref, g_ref, o_ref, q_sc, m_sc, l_sc, acc_sc,
                        *, tq, tk, out_scale):
    i = pl.program_id(1)
    n_heads = q_sc.shape[0]
    _stack_halves(q_ref, q_sc, tq)
    m_sc[...] = jnp.full(m_sc.shape, NEG_BIG, F32)
    l_sc[...] = jnp.zeros(l_sc.shape, F32)
    acc_sc[...] = jnp.zeros(acc_sc.shape, F32)

    def step(j, masked):
        start = pl.multiple_of(j * tk, tk)
        for h in range(n_heads):
            ls = slice(h * LANES, (h + 1) * LANES)
            kb = k_ref[0, pl.ds(start, tk), ls]
            vb = v_ref[0, pl.ds(start, tk), ls]
            s = _dot_t(q_sc[h], kb)
            if masked:
                row = lax.broadcasted_iota(jnp.int32, s.shape, 0) % tq
                col = lax.broadcasted_iota(jnp.int32, s.shape, 1)
                s = jnp.where(j * tk + col <= i * tq + row, s, NEG_BIG)
            m = m_sc[h]
            m_new = jnp.maximum(m, jnp.max(s, axis=-1, keepdims=True))
            alpha = jnp.exp(m - m_new)
            p = jnp.exp(s - m_new)
            l_sc[h] = alpha * l_sc[h] + jnp.sum(p, axis=-1, keepdims=True)
            acc_sc[h] = alpha * acc_sc[h] + _dot(p.astype(BF16), vb)
            m_sc[h] = m_new

    n_full = i * (tq // tk)

    @pl.loop(0, n_full)
    def _(j):
        step(j, False)

    for d in range(tq // tk):
        step(n_full + d, True)

    lam = lam_ref[0, 0]
    for h in range(n_heads):
        o_maps = acc_sc[h] / l_sc[h]
        o = o_maps[0:tq] - lam * o_maps[tq:2 * tq]
        ms = jnp.mean(o * o, axis=-1, keepdims=True)
        o_ref[0, :, h * LANES:(h + 1) * LANES] = (
            o * lax.rsqrt(ms + NORM_EPS) * g_ref[...] * out_scale).astype(o_ref.dtype)


def _diff_prompt_bounded_kernel(lam_ref, kmax_ref, q_ref, k_ref, v_ref, g_ref, o_ref, q_sc, c_sc, acc_sc,
                                *, tq, tk, out_scale):
    bi = pl.program_id(0)
    i = pl.program_id(1)
    n_heads = q_sc.shape[0]
    _stack_halves(q_ref, q_sc, tq)
    first_map = lax.broadcasted_iota(jnp.int32, (2 * tq, 1), 0) < tq
    for h in range(n_heads):
        qf = q_sc[h].astype(F32)
        q_norm = jnp.sqrt(jnp.sum(qf * qf, axis=-1, keepdims=True))
        k_norm = jnp.where(first_map, kmax_ref[bi, 2 * h], kmax_ref[bi, 2 * h + 1])
        c_sc[h] = q_norm * k_norm * BOUND_SLACK
    acc_sc[...] = jnp.zeros(acc_sc.shape, F32)
    ones = jnp.ones((tk, LANES), BF16)

    def step(j, masked):
        start = pl.multiple_of(j * tk, tk)
        for h in range(n_heads):
            ls = slice(h * LANES, (h + 1) * LANES)
            kb = k_ref[0, pl.ds(start, tk), ls]
            vb = v_ref[0, pl.ds(start, tk), ls]
            s = _dot_t(q_sc[h], kb)
            if masked:
                row = lax.broadcasted_iota(jnp.int32, s.shape, 0) % tq
                col = lax.broadcasted_iota(jnp.int32, s.shape, 1)
                s = jnp.where(j * tk + col <= i * tq + row, s, NEG_BIG)
            p = jnp.exp(s - c_sc[h]).astype(BF16)
            acc_sc[h] += _dot(p, jnp.concatenate([vb, ones], axis=1))

    n_full = i * (tq // tk)

    @pl.loop(0, n_full)
    def _(j):
        step(j, False)

    for d in range(tq // tk):
        step(n_full + d, True)

    lam = lam_ref[0, 0]
    for h in range(n_heads):
        acc = acc_sc[h]
        o_maps = acc[:, 0:LANES] / acc[:, LANES:2 * LANES]
        o = o_maps[0:tq] - lam * o_maps[tq:2 * tq]
        ms = jnp.mean(o * o, axis=-1, keepdims=True)
        o_ref[0, :, h * LANES:(h + 1) * LANES] = (
            o * lax.rsqrt(ms + NORM_EPS) * g_ref[...] * out_scale).astype(o_ref.dtype)


def _diff_prompt_bounded(lam, kmax, q, k, v, g_subln, out_scale, tq, tk):
    b, s, w = q.shape
    n_heads = w // LANES
    kern = functools.partial(_diff_prompt_bounded_kernel, tq=tq, tk=tk, out_scale=out_scale)
    return pl.pallas_call(
        kern,
        grid=(b, s // tq),
        in_specs=[
            pl.BlockSpec(memory_space=pltpu.SMEM),
            pl.BlockSpec(memory_space=pltpu.SMEM),
            pl.BlockSpec((1, tq, w), lambda bi, i: (bi, i, 0)),
            pl.BlockSpec((1, s, w), lambda bi, i: (bi, 0, 0)),
            pl.BlockSpec((1, s, w), lambda bi, i: (bi, 0, 0)),
            pl.BlockSpec((1, LANES), lambda bi, i: (0, 0)),
        ],
        out_specs=pl.BlockSpec((1, tq, w), lambda bi, i: (bi, i, 0)),
        out_shape=jax.ShapeDtypeStruct((b, s, w), BF16),
        scratch_shapes=[
            pltpu.VMEM((n_heads, 2 * tq, LANES), BF16),
            pltpu.VMEM((n_heads, 2 * tq, 1), F32),
            pltpu.VMEM((n_heads, 2 * tq, 2 * LANES), F32),
        ],
        compiler_params=_cparams(("arbitrary", "arbitrary")),
        name="diff_prompt_bounded",
    )(lam, kmax, q, k, v, g_subln)


def _head_norm_max(a, group):
    b, s, w = a.shape
    af = a.astype(F32).reshape(b, s, w // group, group)
    return jnp.sqrt(jnp.max(jnp.sum(af * af, axis=-1), axis=1))


def _diff_prompt(lam, q, k, v, g_subln, out_scale, tq, tk):
    b, s, w = q.shape
    n_heads = w // LANES
    kern = functools.partial(_diff_prompt_kernel, tq=tq, tk=tk, out_scale=out_scale)
    return pl.pallas_call(
        kern,
        grid=(b, s // tq),
        in_specs=[
            pl.BlockSpec(memory_space=pltpu.SMEM),
            pl.BlockSpec((1, tq, w), lambda bi, i: (bi, i, 0)),
            pl.BlockSpec((1, s, w), lambda bi, i: (bi, 0, 0)),
            pl.BlockSpec((1, s, w), lambda bi, i: (bi, 0, 0)),
            pl.BlockSpec((1, LANES), lambda bi, i: (0, 0)),
        ],
        out_specs=pl.BlockSpec((1, tq, w), lambda bi, i: (bi, i, 0)),
        out_shape=jax.ShapeDtypeStruct((b, s, w), BF16),
        scratch_shapes=[
            pltpu.VMEM((n_heads, 2 * tq, LANES), BF16),
            pltpu.VMEM((n_heads, 2 * tq, 1), F32),
            pltpu.VMEM((n_heads, 2 * tq, 1), F32),
            pltpu.VMEM((n_heads, 2 * tq, LANES), F32),
        ],
        compiler_params=_cparams(("arbitrary", "arbitrary")),
        name="diff_prompt",
    )(lam, q, k, v, g_subln)


def _sb_block(z, valid, tri_ref, r_in):
    log_1m = jnp.minimum(-z, 0.0) - jnp.log(1.0 + jnp.exp(-jnp.abs(z)))
    if valid is not None:
        log_1m = jnp.where(valid, log_1m, 0.0)
    later = _dot(jnp.concatenate(_split_bf16(log_1m), axis=1), tri_ref[...])
    a = jnp.exp((z + log_1m) + (later + r_in))
    if valid is not None:
        a = jnp.where(valid, a, 0.0)
    r_out = r_in + later[:, 0:1] + log_1m[:, 0:1]
    return a, r_out


def _head_rms(o, seg_ref, g):
    ms = _dot((o * o).astype(BF16), seg_ref[...])
    return o * lax.rsqrt(ms + NORM_EPS) * g


def _sb_prompt_kernel(q_ref, k_ref, v_ref, g_ref, tri_ref, seg_ref, o_ref, q_sc, r_sc, acc_sc, *, tq, tk):
    i = pl.program_id(1)
    n_blocks = q_sc.shape[0]
    _stack_halves(q_ref, q_sc, tq)
    r_sc[...] = jnp.zeros(r_sc.shape, F32)
    acc_sc[...] = jnp.zeros(acc_sc.shape, F32)

    def step(j, masked):
        start = pl.multiple_of(j * tk, tk)
        r_max = jnp.full((1, 1), NEG_BIG, F32)
        for p in range(n_blocks):
            ls = slice(p * LANES, (p + 1) * LANES)
            kb = k_ref[0, pl.ds(start, tk), ls]
            vb = v_ref[0, pl.ds(start, tk), ls]
            z = _dot_t(q_sc[p], kb)
            valid = None
            if masked:
                row = lax.broadcasted_iota(jnp.int32, z.shape, 0) % tq
                col = lax.broadcasted_iota(jnp.int32, z.shape, 1)
                valid = j * tk + col < i * tq + row
            a, r = _sb_block(z, valid, tri_ref, r_sc[p])
            acc_sc[p] += _dot(a.astype(BF16), vb)
            r_sc[p] = r
            r_max = jnp.maximum(r_max, jnp.max(r, axis=0, keepdims=True))
        return (r_max[0, 0] >= SB_DONE).astype(jnp.int32)

    n_full = i * (tq // tk)
    go = jnp.int32(1)
    for d in reversed(range(tq // tk)):
        go = step(n_full + d, True)

    def not_done(c):
        return jnp.logical_and(c[0] >= 0, c[1] > 0)

    def body(c):
        return c[0] - 1, step(c[0], False)

    lax.while_loop(not_done, body, (n_full - 1, go))
    lo_half = _half_masks((tq, LANES))
    for p in range(n_blocks):
        acc = acc_sc[p]
        o = jnp.where(lo_half, acc[0:tq], acc[tq:2 * tq])
        o_ref[0, :, p * LANES:(p + 1) * LANES] = _head_rms(o, seg_ref, g_ref[...]).astype(o_ref.dtype)


def _sb_prompt(q, k, v, g_sb, tri, seg, tq, tk):
    b, s, w = q.shape
    n_blocks = w // LANES
    kern = functools.partial(_sb_prompt_kernel, tq=tq, tk=tk)
    return pl.pallas_call(
        kern,
        grid=(b, s // tq),
        in_specs=[
            pl.BlockSpec((1, tq, w), lambda bi, i: (bi, i, 0)),
            pl.BlockSpec((1, s, w), lambda bi, i: (bi, 0, 0)),
            pl.BlockSpec((1, s, w), lambda bi, i: (bi, 0, 0)),
            pl.BlockSpec((1, LANES), lambda bi, i: (0, 0)),
            pl.BlockSpec(tri.shape, lambda bi, i: (0, 0)),
            pl.BlockSpec(seg.shape, lambda bi, i: (0, 0)),
        ],
        out_specs=pl.BlockSpec((1, tq, w), lambda bi, i: (bi, i, 0)),
        out_shape=jax.ShapeDtypeStruct((b, s, w), BF16),
        scratch_shapes=[
            pltpu.VMEM((n_blocks, 2 * tq, LANES), BF16),
            pltpu.VMEM((n_blocks, 2 * tq, 1), F32),
            pltpu.VMEM((n_blocks, 2 * tq, LANES), F32),
        ],
        compiler_params=_cparams(("arbitrary", "arbitrary")),
        name="sb_prompt",
    )(q, k, v, g_sb, tri, seg)


Q_ROWS = 16


def _page_spec(n_pages, first_of_chunk, r, page_shape):
    def index_map(b, c, pt_ref):
        return (pt_ref[b * n_pages + first_of_chunk(c) + r], 0, 0)
    return pl.BlockSpec((1,) + page_shape, index_map)


def _page_scores(q, k_refs, rows):
    return jnp.concatenate([_dot(q, r[0, rows, :].astype(BF16)) for r in k_refs], axis=1)


def _diff_sample_kernel(pt_ref, lam_ref, q_ref, kn_ref, vn_ref, g_ref, *rest, n_pp, t_new, out_scale):
    k_refs, v_refs = rest[:n_pp], rest[n_pp:2 * n_pp]
    o_ref, m_sc, l_sc, acc_sc = rest[2 * n_pp:]
    c = pl.program_id(1)
    n_heads = q_ref.shape[-1] // LANES

    @pl.when(c == 0)
    def _():
        m_sc[...] = jnp.full(m_sc.shape, NEG_BIG, F32)
        l_sc[...] = jnp.zeros(l_sc.shape, F32)
        acc_sc[...] = jnp.zeros(acc_sc.shape, F32)

    def update(h, s, pv):
        m = m_sc[h]
        m_new = jnp.maximum(m, jnp.max(s, axis=-1, keepdims=True))
        alpha = jnp.exp(m - m_new)
        p = jnp.exp(s - m_new)
        l_sc[h] = alpha * l_sc[h] + jnp.sum(p, axis=-1, keepdims=True)
        acc_sc[h] = alpha * acc_sc[h] + pv(p.astype(BF16))
        m_sc[h] = m_new

    page_size = k_refs[0].shape[-1]
    for h in range(n_heads):
        ls = slice(h * LANES, (h + 1) * LANES)

        def pv(p, h=h):
            out = jnp.zeros((Q_ROWS, LANES), F32)
            for r, v_r in enumerate(v_refs):
                vb = v_r[0, pl.ds(h, page_size, stride=n_heads), :].astype(BF16)
                out += _dot(p[:, r * page_size:(r + 1) * page_size], vb)
            return out

        update(h, _page_scores(q_ref[0, :, ls], k_refs, ls), pv)

    @pl.when(c == pl.num_programs(1) - 1)
    def _():
        lam = lam_ref[0, 0]
        row = lax.broadcasted_iota(jnp.int32, (Q_ROWS, Q_ROWS), 0)
        col = lax.broadcasted_iota(jnp.int32, (Q_ROWS, Q_ROWS), 1)
        valid = jnp.logical_and(col < t_new, col <= row % t_new)
        for h in range(n_heads):
            ls = slice(h * LANES, (h + 1) * LANES)
            s = jnp.where(valid, _dot_t(q_ref[0, :, ls], kn_ref[0, :, ls]), NEG_BIG)
            update(h, s, lambda p, ls=ls: _dot(p, vn_ref[0, :, ls]))
            o_maps = acc_sc[h] / l_sc[h]
            o = o_maps[0:8] - lam * o_maps[t_new:t_new + 8]
            ms = jnp.mean(o * o, axis=-1, keepdims=True)
            o_ref[0, :, ls] = o * lax.rsqrt(ms + NORM_EPS) * g_ref[...] * out_scale


def _diff_sample(pt_flat, lam, q16, k_new, v_new, g_subln, cache_k, cache_v, n_pages, n_pp, t_new, out_scale):
    nb, _, w = q16.shape
    n_heads = w // LANES
    per_seq = lambda b, c, pt: (b, 0, 0)
    chunk0 = lambda c: c * n_pp
    pages = lambda cache: [_page_spec(n_pages, chunk0, r, cache.shape[1:]) for r in range(n_pp)]
    kern = functools.partial(_diff_sample_kernel, n_pp=n_pp, t_new=t_new, out_scale=out_scale)
    grid_spec = pltpu.PrefetchScalarGridSpec(
        num_scalar_prefetch=1,
        grid=(nb, n_pages // n_pp),
        in_specs=[
            pl.BlockSpec(memory_space=pltpu.SMEM),
            pl.BlockSpec((1, Q_ROWS, w), per_seq),
            pl.BlockSpec((1, Q_ROWS, w), per_seq),
            pl.BlockSpec((1, Q_ROWS, w), per_seq),
            pl.BlockSpec((1, LANES), lambda b, c, pt: (0, 0)),
        ] + pages(cache_k) + pages(cache_v),
        out_specs=pl.BlockSpec((1, 8, w), per_seq),
        scratch_shapes=[
            pltpu.VMEM((n_heads, Q_ROWS, 1), F32),
            pltpu.VMEM((n_heads, Q_ROWS, 1), F32),
            pltpu.VMEM((n_heads, Q_ROWS, LANES), F32),
        ],
    )
    return pl.pallas_call(
        kern,
        grid_spec=grid_spec,
        out_shape=jax.ShapeDtypeStruct((nb, 8, w), F32),
        compiler_params=_cparams(("arbitrary", "arbitrary")),
        name="diff_sample",
    )(pt_flat, lam, q16, k_new, v_new, g_subln, *([cache_k] * n_pp), *([cache_v] * n_pp))


def _sb_sample_kernel(pt_ref, q_ref, kn_ref, vn_ref, g_ref, tri_ref, trin_ref, seg_ref, acc_in, r_in, *rest,
                      n_pp, t_new, with_new):
    k_refs, v_refs = rest[:n_pp], rest[n_pp:2 * n_pp]
    o_ref, acc_ref, r_ref = rest[2 * n_pp:]
    c = pl.program_id(1)
    n_blocks = q_ref.shape[-1] // LANES

    @pl.when(c == 0)
    def _():
        if with_new:
            row = lax.broadcasted_iota(jnp.int32, (Q_ROWS, Q_ROWS), 0)
            col = lax.broadcasted_iota(jnp.int32, (Q_ROWS, Q_ROWS), 1)
            valid = jnp.logical_and(col < t_new, col < row % t_new)
            for p in range(n_blocks):
                ls = slice(p * LANES, (p + 1) * LANES)
                z = _dot_t(q_ref[0, :, ls], kn_ref[0, :, ls])
                a, r = _sb_block(z, valid, trin_ref, jnp.zeros((Q_ROWS, 1), F32))
                acc_ref[0, :, ls] = _dot(a.astype(BF16), vn_ref[0, :, ls])
                r_ref[0, :, ls] = jnp.broadcast_to(r, (Q_ROWS, LANES))
        else:
            acc_ref[...] = acc_in[...]
            r_ref[...] = r_in[...]

    page_size = k_refs[0].shape[-1]
    for p in range(n_blocks):
        ls = slice(p * LANES, (p + 1) * LANES)
        z = _page_scores(q_ref[0, :, ls], k_refs, ls)
        a, r = _sb_block(z, None, tri_ref, r_ref[0, :, p * LANES:p * LANES + 1])
        a = a.astype(BF16)
        pv = jnp.zeros((Q_ROWS, LANES), F32)
        for i, v_r in enumerate(v_refs):
            pv += _dot_t(a[:, i * page_size:(i + 1) * page_size], v_r[0, ls, :].astype(BF16))
        acc_ref[0, :, ls] += pv
        r_ref[0, :, ls] = jnp.broadcast_to(r, (Q_ROWS, LANES))

    @pl.when(c == pl.num_programs(1) - 1)
    def _():
        lo_half = _half_masks((8, LANES))
        for p in range(n_blocks):
            ls = slice(p * LANES, (p + 1) * LANES)
            acc = acc_ref[0, :, ls]
            o = jnp.where(lo_half, acc[0:8], acc[t_new:t_new + 8])
            o_ref[0, :, ls] = _head_rms(o, seg_ref, g_ref[...])


def _sb_sample(pt_flat, q16, k_new, v_new, g_sb, tri, tri_new, seg, acc_in, r_in, cache_k, cache_v,
               n_pages, page_hi, n_chunks, n_pp, t_new, with_new):
    nb, _, w = q16.shape
    per_seq = lambda b, c, pt: (b, 0, 0)
    fixed = lambda b, c, pt: (0, 0)
    first = lambda c: page_hi - (c + 1) * n_pp
    pages = lambda cache: [_page_spec(n_pages, first, r, cache.shape[1:]) for r in range(n_pp)]
    kern = functools.partial(_sb_sample_kernel, n_pp=n_pp, t_new=t_new, with_new=with_new)
    state_spec = pl.BlockSpec((1, Q_ROWS, w), per_seq)
    grid_spec = pltpu.PrefetchScalarGridSpec(
        num_scalar_prefetch=1,
        grid=(nb, n_chunks),
        in_specs=[
            state_spec, state_spec, state_spec,
            pl.BlockSpec((1, LANES), fixed),
            pl.BlockSpec(tri.shape, fixed),
            pl.BlockSpec(tri_new.shape, fixed),
            pl.BlockSpec(seg.shape, fixed),
            state_spec, state_spec,
        ] + pages(cache_k) + pages(cache_v),
        out_specs=[pl.BlockSpec((1, 8, w), per_seq), state_spec, state_spec],
    )
    state = jax.ShapeDtypeStruct((nb, Q_ROWS, w), F32)
    return pl.pallas_call(
        kern,
        grid_spec=grid_spec,
        out_shape=[jax.ShapeDtypeStruct((nb, 8, w), F32), state, state],
        compiler_params=_cparams(("arbitrary", "arbitrary")),
        name="sb_sample_new" if with_new else "sb_sample_rest",
    )(pt_flat, q16, k_new, v_new, g_sb, tri, tri_new, seg, acc_in, r_in,
      *([cache_k] * n_pp), *([cache_v] * n_pp))


def _pack_bf16_pairs(x):
    c = x.shape[-1] // 2
    bits = lax.bitcast_convert_type(x.astype(F32), jnp.uint32)
    return (bits[:, :c] >> 16) | (bits[:, c:] & jnp.uint32(0xFFFF0000))


def _unpack_bf16_pairs(w):
    lo = lax.bitcast_convert_type(w << 16, F32)
    hi = lax.bitcast_convert_type(w & jnp.uint32(0xFFFF0000), F32)
    return jnp.concatenate([lo, hi], axis=1).astype(BF16)


def _finish_kernel(x_ref, od_ref, os_ref, wo_ref, g_ref, wr_hi_ref, wr_lo_ref, br_ref,
                   h_ref, hn_ref, route_ref):
    half = od_ref.shape[-1]
    h = x_ref[...] + _dot(od_ref[...], wo_ref[0:half, :]) + _dot(os_ref[...], wo_ref[half:, :])
    h_ref[...] = h
    ms = jnp.mean(h * h, axis=-1, keepdims=True)
    hn = h * lax.rsqrt(ms + NORM_EPS) * g_ref[...]
    hn_hi, hn_lo = _split_bf16(hn)
    hn_ref[...] = _pack_bf16_pairs(hn_hi)

    logits = (_dot(hn_hi, wr_hi_ref[...]) + _dot(hn_lo, wr_hi_ref[...]) + _dot(hn_hi, wr_lo_ref[...])
              + br_ref[...])
    n_exp = N_GROUPS * EXPERTS_PER_GROUP
    lane = lax.broadcasted_iota(jnp.int32, logits.shape, 1)

    def first_argmax(v, vmax):
        return jnp.min(jnp.where(v == vmax, lane, LANES), axis=-1, keepdims=True)

    lg = jnp.where(jnp.logical_and(lane >= n_exp, lane < n_exp + N_GROUPS), logits, NEG_BIG)
    mg = jnp.max(lg, axis=-1, keepdims=True)
    grp = first_argmax(lg, mg) - n_exp
    p_top = 1.0 / jnp.sum(jnp.exp(lg - mg), axis=-1, keepdims=True)

    le = jnp.where(jnp.logical_and(lane < n_exp, lane // EXPERTS_PER_GROUP == grp), logits, NEG_BIG)
    m1 = jnp.max(le, axis=-1, keepdims=True)
    i1 = first_argmax(le, m1)
    le2 = jnp.where(lane == i1, NEG_BIG, le)
    m2 = jnp.max(le2, axis=-1, keepdims=True)
    i2 = first_argmax(le2, m2)
    e2 = jnp.exp(m2 - m1)
    gate1 = p_top / (1.0 + e2)
    gate2 = p_top * e2 / (1.0 + e2)
    route = jnp.where(lane == 0, i1.astype(F32),
                      jnp.where(lane == 1, i2.astype(F32),
                                jnp.where(lane == 2, gate1, jnp.where(lane == 3, gate2, 0.0))))
    route_ref[...] = route


def _finish(x2d, od, osb, w_o_bf, g_ffn, wr_hi, wr_lo, b_r, tm):
    n, d = x2d.shape
    half = od.shape[-1]
    row = lambda i: (i, 0)
    fixed = lambda i: (0, 0)
    return pl.pallas_call(
        _finish_kernel,
        grid=(n // tm,),
        in_specs=[
            pl.BlockSpec((tm, d), row),
            pl.BlockSpec((tm, half), row),
            pl.BlockSpec((tm, half), row),
            pl.BlockSpec(w_o_bf.shape, fixed),
            pl.BlockSpec((1, d), fixed),
            pl.BlockSpec(wr_hi.shape, fixed),
            pl.BlockSpec(wr_lo.shape, fixed),
            pl.BlockSpec((1, LANES), fixed),
        ],
        out_specs=[pl.BlockSpec((tm, d), row), pl.BlockSpec((tm, d // 2), row), pl.BlockSpec((tm, LANES), row)],
        out_shape=[jax.ShapeDtypeStruct((n, d), F32), jax.ShapeDtypeStruct((n, d // 2), jnp.uint32),
                   jax.ShapeDtypeStruct((n, LANES), F32)],
        compiler_params=_cparams(("arbitrary",)),
        name="out_proj_router",
    )(x2d, od, osb, w_o_bf, g_ffn, wr_hi, wr_lo, b_r)


def _expert_kernel(be_ref, nact_ref, x_ref, wg_ref, wu_ref, wd_ref, y_ref):
    blk = pl.program_id(0)

    @pl.when(blk < nact_ref[0])
    def _():
        x = _unpack_bf16_pairs(x_ref[...])
        gate = _dot(x, wg_ref[0])
        up = _dot(x, wu_ref[0])
        hidden = (gate * jax.nn.sigmoid(gate) * up).astype(BF16)
        y_ref[...] = _dot(hidden, wd_ref[0])

    @pl.when(blk >= nact_ref[0])
    def _():
        y_ref[...] = jnp.zeros(y_ref.shape, y_ref.dtype)


def _expert_ffn(block_exp, n_active, xs, w_gate, w_up, w_down, bm):
    n_slots = xs.shape[0]
    d, ff = w_gate.shape[-2:]
    grid_spec = pltpu.PrefetchScalarGridSpec(
        num_scalar_prefetch=2,
        grid=(n_slots // bm,),
        in_specs=[
            pl.BlockSpec((bm, xs.shape[1]), lambda i, be, na: (i, 0)),
            pl.BlockSpec((1, d, ff), lambda i, be, na: (be[i], 0, 0)),
            pl.BlockSpec((1, d, ff), lambda i, be, na: (be[i], 0, 0)),
            pl.BlockSpec((1, ff, d), lambda i, be, na: (be[i], 0, 0)),
        ],
        out_specs=pl.BlockSpec((bm, d), lambda i, be, na: (i, 0)),
    )
    return pl.pallas_call(
        _expert_kernel,
        grid_spec=grid_spec,
        out_shape=jax.ShapeDtypeStruct((n_slots, d), F32),
        compiler_params=_cparams(("arbitrary",)),
        name="expert_ffn",
    )(block_exp, n_active, xs, w_gate, w_up, w_down)


def _moe(h, hn_packed, route, w_gate, w_up, w_down, bm):
    n_tok, d = h.shape
    n_exp = w_gate.shape[0]
    ids = route[:, 0:2].astype(jnp.int32)
    gates = route[:, 2:4]
    n_assign = 2 * n_tok
    flat_e = ids.reshape(-1)
    order = jnp.argsort(flat_e, stable=True)
    e_sorted = flat_e[order]
    counts = jnp.zeros((n_exp,), jnp.int32).at[flat_e].add(1)
    padded = (counts + bm - 1) // bm * bm
    start = jnp.cumsum(counts) - counts
    pend = jnp.cumsum(padded)
    pstart = pend - padded
    dest_sorted = pstart[e_sorted] + (jnp.arange(n_assign, dtype=jnp.int32) - start[e_sorted])
    n_blocks = -(-n_assign // bm) + n_exp
    n_slots = n_blocks * bm
    slot_tok = jnp.zeros((n_slots,), jnp.int32).at[dest_sorted].set(order // 2)
    dest = jnp.zeros((n_assign,), jnp.int32).at[order].set(dest_sorted).reshape(n_tok, 2)
    block_exp = jnp.minimum(
        jnp.searchsorted(pend, jnp.arange(n_blocks, dtype=jnp.int32) * bm, side='right'), n_exp - 1
    ).astype(jnp.int32)
    n_active = (pend[-1:] // bm).astype(jnp.int32)
    xs = hn_packed[slot_tok]
    yb = _expert_ffn(block_exp, n_active, xs, w_gate, w_up, w_down, bm)
    return h + (gates[:, 0:1] * yb[dest[:, 0]] + gates[:, 1:2] * yb[dest[:, 1]])


def _block_diag_mean(width, group):
    idx = jnp.arange(width) // group
    return jnp.where(idx[:, None] == idx[None, :], 1.0 / group, 0.0).astype(BF16)


def _tri(n):
    idx = jnp.arange(n)
    tri = (idx[:, None] > idx[None, :]).astype(BF16)
    return jnp.concatenate([tri, tri], axis=0)


def _sample_rows(a, nb, t):
    w = a.shape[-1]
    a = a.reshape(nb, t, w)
    lo = (jnp.arange(w) % LANES) < DH
    zeros = jnp.zeros((nb, Q_ROWS - 2 * t, w), a.dtype)
    return jnp.concatenate([jnp.where(lo, a, 0), jnp.where(lo, 0, a), zeros], axis=1)


def _pad_rows(a, nb, t):
    w = a.shape[-1]
    a = a.reshape(nb, t, w)
    return jnp.concatenate([a, jnp.zeros((nb, Q_ROWS - t, w), a.dtype)], axis=1)


def kernel(x_prompt, x_sample, cache_k_diff, cache_v_diff, cache_k_sb, cache_v_sb, page_table, g_attn_norm, w_in, g_q_norm, g_k_norm, lambda_q1, lambda_k1, lambda_q2, lambda_k2, g_subln, g_sb_out, w_o, g_ffn_norm, w_router_group, b_router_group, w_router_expert, b_router_expert, w_exp_gate, w_exp_up, w_exp_down):
    depth = w_in.shape[0]
    b, s, d = x_prompt.shape
    nb, t_new, _ = x_sample.shape
    n_phys, page_size = cache_k_diff.shape[1], cache_k_diff.shape[2]
    n_pages = page_table.shape[1]
    past_len = n_pages * page_size
    width = w_in.shape[-1] // 6
    n_exp = w_exp_gate.shape[1]

    tm = min(256, nb * t_new)
    tq, tk = min(PROMPT_TQ, s), min(PROMPT_TK, s)
    n_pp_diff = min(16, n_pages)
    n_pp_sb = min(4, n_pages)
    bm = 256

    seg64 = _block_diag_mean(width, DH)
    seg64_lane = _block_diag_mean(LANES, DH)
    tri_p = _tri(tk)
    tri_s = _tri(n_pp_sb * page_size)
    tri_new = _tri(Q_ROWS)
    pos_p = _rope_tables(jnp.arange(s), 1)
    pos_s = _rope_tables(past_len + jnp.arange(t_new), tm // t_new)

    hp = x_prompt.reshape(b * s, d)
    hs = x_sample.reshape(nb * t_new, d)
    outs = [[] for _ in range(8)]
    for layer in range(depth):
        lam_init = 0.8 - 0.6 * math.exp(-0.3 * layer)
        lam = (jnp.exp(jnp.sum(lambda_q1[layer].astype(F32) * lambda_k1[layer].astype(F32)))
               - jnp.exp(jnp.sum(lambda_q2[layer].astype(F32) * lambda_k2[layer].astype(F32))) + lam_init)
        lam = lam.reshape(1, 1).astype(F32)
        out_scale = 1.0 - lam_init
        g_attn = g_attn_norm[layer].reshape(1, d)
        w_in_bf = w_in[layer].astype(BF16)
        gq = jnp.tile(g_q_norm[layer], width // DH).reshape(1, width)
        gk = jnp.tile(g_k_norm[layer], width // DH).reshape(1, width)
        g_sub = g_subln[layer].reshape(1, LANES)
        g_sb = jnp.tile(g_sb_out[layer], LANES // DH).reshape(1, LANES)

        (kd_p, vd_p, ks_p, vs_p, qd_pb, kd_pb, vd_pb, qs_pb, ks_pb, vs_pb) = _project(
            hp, pos_p, g_attn, w_in_bf, gq, gk, seg64, min(256, s))
        (kd_s, vd_s, ks_s, vs_s, qd_sb, kd_sb, vd_sb, qs_sb, ks_sb, vs_sb) = _project(
            hs, pos_s, g_attn, w_in_bf, gq, gk, seg64, tm)

        r3 = lambda a: a.reshape(b, s, width)
        qd3, kd3, vd3 = r3(qd_pb), r3(kd_pb), r3(vd_pb)
        qmax, kmax = _head_norm_max(qd3, DH), _head_norm_max(kd3, DH)
        od_p = lax.cond(
            jnp.max(qmax * kmax) * BOUND_SLACK <= BOUND_MAX,
            lambda: _diff_prompt_bounded(lam, kmax, qd3, kd3, vd3, g_sub, out_scale, tq, tk),
            lambda: _diff_prompt(lam, qd3, kd3, vd3, g_sub, out_scale, tq, tk))
        osb_p = _sb_prompt(r3(qs_pb), r3(ks_pb), r3(vs_pb), g_sb, tri_p, seg64_lane, tq, tk)

        pt_flat = (page_table.astype(jnp.int32) + layer * n_phys).reshape(-1)
        transposed_pages = lambda c: jnp.transpose(c, (0, 1, 3, 4, 2)).reshape(depth * n_phys, width, page_size)
        v_diff_pages = cache_v_diff.reshape(depth * n_phys, page_size * (width // DV_DIFF_W), DV_DIFF_W)
        od_s = _diff_sample(pt_flat, lam, _sample_rows(qd_sb, nb, t_new), _pad_rows(kd_sb, nb, t_new),
                            _pad_rows(vd_sb, nb, t_new), g_sub, transposed_pages(cache_k_diff),
                            v_diff_pages, n_pages, n_pp_diff, t_new, out_scale)
        q16s = _sample_rows(qs_sb, nb, t_new)
        kn16, vn16 = _pad_rows(ks_sb, nb, t_new), _pad_rows(vs_sb, nb, t_new)
        cks, cvs = transposed_pages(cache_k_sb), transposed_pages(cache_v_sb)
        zeros_state = jnp.zeros((nb, Q_ROWS, width), F32)
        sb_args = (pt_flat, q16s, kn16, vn16, g_sb, tri_s, tri_new, seg64_lane)
        osb_s, acc_s, r_s = _sb_sample(*sb_args, zeros_state, zeros_state, cks, cvs,
                                       n_pages, n_pages, 1, n_pp_sb, t_new, True)
        n_rest = (n_pages - n_pp_sb) // n_pp_sb
        if n_rest > 0:
            unfinished = jnp.max(r_s[:, :2 * t_new, :]) >= SB_DONE
            osb_s = lax.cond(
                unfinished,
                lambda: _sb_sample(*sb_args, acc_s, r_s, cks, cvs, n_pages, n_pages - n_pp_sb,
                                   n_rest, n_pp_sb, t_new, False)[0],
                lambda: osb_s)

        w_o_bf = w_o[layer].astype(BF16)
        g_ffn = g_ffn_norm[layer].reshape(1, d)
        w_r = jnp.zeros((d, LANES), F32)
        w_r = w_r.at[:, :n_exp].set(w_router_expert[layer]).at[:, n_exp:n_exp + N_GROUPS].set(w_router_group[layer])
        wr_hi, wr_lo = _split_bf16(w_r)
        b_r = jnp.zeros((1, LANES), F32)
        b_r = b_r.at[0, :n_exp].set(b_router_expert[layer]).at[0, n_exp:n_exp + N_GROUPS].set(b_router_group[layer])

        od_s2 = od_s[:, :t_new].reshape(nb * t_new, width).astype(BF16)
        osb_s2 = osb_s[:, :t_new].reshape(nb * t_new, width).astype(BF16)
        fin = lambda xx, od, osb, tile: _finish(xx, od, osb, w_o_bf, g_ffn, wr_hi, wr_lo, b_r, tile)
        h_p, hn_p, route_p = fin(hp, od_p.reshape(b * s, width), osb_p.reshape(b * s, width), min(256, s))
        h_s, hn_s, route_s = fin(hs, od_s2, osb_s2, tm)

        y = _moe(jnp.concatenate([h_p, h_s]), jnp.concatenate([hn_p, hn_s]), jnp.concatenate([route_p, route_s]),
                 w_exp_gate[layer].astype(BF16), w_exp_up[layer].astype(BF16), w_exp_down[layer].astype(BF16), bm)
        hp, hs = y[:b * s], y[b * s:]

        for lst, val in zip(outs, (kd_p, vd_p, ks_p, vs_p, kd_s, vd_s, ks_s, vs_s)):
            lst.append(val)

    h_diff, h_sb = width // DV_DIFF_W, width // DH
    shapes_p = [(b, s, 2 * h_diff, DH), (b, s, h_diff, DV_DIFF_W), (b, s, h_sb, DH), (b, s, h_sb, DH)]
    shapes_s = [(nb, t_new) + sh[2:] for sh in shapes_p]
    stacked = [jnp.stack([v.reshape(sh) for v in lst]) for lst, sh in zip(outs, shapes_p + shapes_s)]
    return (hp.reshape(b, s, d), hs.reshape(nb, t_new, d), *stacked)
```

```python
import functools
import math

import jax
import jax.numpy as jnp
from jax import lax
from jax.experimental import pallas as pl
from jax.experimental.pallas import tpu as pltpu

F32 = jnp.float32
BF16 = jnp.bfloat16

NORM_EPS = 1e-6
ROPE_THETA = 10000.0
DH = 64
DV_DIFF_W = 2 * DH
LANES = 128
N_GROUPS = 4
EXPERTS_PER_GROUP = 8
NEG_BIG = -1e30
SB_DONE = -110.0
VMEM_LIMIT = 56 * 1024 * 1024
BOUND_MAX = 40.0
BOUND_SLACK = 1.01
PROMPT_TQ = 256
PROMPT_TK = 256


def _cparams(sem):
    return pltpu.CompilerParams(dimension_semantics=sem, vmem_limit_bytes=VMEM_LIMIT)


def _dot_t(a, b):
    return lax.dot_general(a, b, (((1,), (1,)), ((), ())), preferred_element_type=F32)


def _dot(a, b):
    return jnp.dot(a, b, preferred_element_type=F32)


def _split_bf16(x):
    hi = x.astype(BF16)
    lo = (x - hi.astype(F32)).astype(BF16)
    return hi, lo


def _proj_kernel(x_ref, g_ref, w_ref, gq_ref, gk_ref, cos_ref, sin_ref, seg_ref,
                 kd_ref, vd_ref, ks_ref, vs_ref,
                 qd_bf, kd_bf, vd_bf, qs_bf, ks_bf, vs_bf):
    x = x_ref[...]
    ms = jnp.mean(x * x, axis=-1, keepdims=True)
    xn = (x * lax.rsqrt(ms + NORM_EPS) * g_ref[...]).astype(BF16)
    width = kd_ref.shape[-1]

    def seg(c):
        return _dot(xn, w_ref[:, c * width:(c + 1) * width])

    reps = width // LANES
    cos = jnp.concatenate([cos_ref[...]] * reps, axis=1)
    sin = jnp.concatenate([sin_ref[...]] * reps, axis=1)
    lane = lax.broadcasted_iota(jnp.int32, (x.shape[0], width), 1)
    first_half = (lane % DH) < (DH // 2)

    def norm_rope(p, g):
        ms_h = _dot((p * p).astype(BF16), seg_ref[...])
        y = p * lax.rsqrt(ms_h + NORM_EPS) * g
        partner = jnp.where(first_half, pltpu.roll(y, width - DH // 2, 1), pltpu.roll(y, DH // 2, 1))
        return y * cos + partner * sin

    scale = DH ** -0.5
    qd = norm_rope(seg(0), gq_ref[...])
    qd_bf[...] = (qd * scale).astype(BF16)
    kd = norm_rope(seg(1), gk_ref[...])
    kd_ref[...] = kd
    kd_bf[...] = kd.astype(BF16)
    vd = seg(2)
    vd_ref[...] = vd
    vd_bf[...] = vd.astype(BF16)
    qs_bf[...] = (seg(3) * scale).astype(BF16)
    ks = seg(4)
    ks_ref[...] = ks
    ks_bf[...] = ks.astype(BF16)
    vs = seg(5)
    vs_ref[...] = vs
    vs_bf[...] = vs.astype(BF16)


def _rope_tables(pos, reps):
    half = DH // 2
    inv = jnp.power(jnp.float32(ROPE_THETA), -(jnp.arange(half, dtype=F32) * 2.0 / DH))
    ang = pos.astype(F32)[:, None] * inv[None, :]
    cos, sin = jnp.cos(ang), jnp.sin(ang)
    cos_t = jnp.concatenate([cos, cos] * (LANES // DH), axis=1)
    sin_t = jnp.concatenate([-sin, sin] * (LANES // DH), axis=1)
    return jnp.tile(cos_t, (reps, 1)), jnp.tile(sin_t, (reps, 1))


def _project(x2d, pos_tables, g_attn, w_in_bf, gq, gk, seg_mat, tm):
    n, d = x2d.shape
    width = w_in_bf.shape[1] // 6
    cos_t, sin_t = pos_tables
    n_pos_blocks = cos_t.shape[0] // tm
    row = lambda i: (i, 0)
    fixed = lambda i: (0, 0)
    out_f32 = jax.ShapeDtypeStruct((n, width), F32)
    out_bf = jax.ShapeDtypeStruct((n, width), BF16)
    return pl.pallas_call(
        _proj_kernel,
        grid=(n // tm,),
        in_specs=[
            pl.BlockSpec((tm, d), row),
            pl.BlockSpec((1, d), fixed),
            pl.BlockSpec(w_in_bf.shape, fixed),
            pl.BlockSpec((1, width), fixed),
            pl.BlockSpec((1, width), fixed),
            pl.BlockSpec((tm, LANES), lambda i: (i % n_pos_blocks, 0)),
            pl.BlockSpec((tm, LANES), lambda i: (i % n_pos_blocks, 0)),
            pl.BlockSpec(seg_mat.shape, fixed),
        ],
        out_specs=[pl.BlockSpec((tm, width), row)] * 10,
        out_shape=[out_f32] * 4 + [out_bf] * 6,
        compiler_params=_cparams(("arbitrary",)),
        name="in_proj",
    )(x2d, g_attn, w_in_bf, gq, gk, cos_t, sin_t, seg_mat)


def _half_masks(shape):
    lane = lax.broadcasted_iota(jnp.int32, shape, 1)
    return lane < DH


def _stack_halves(q_ref, q_sc, tq):
    lo_half = _half_masks((tq, LANES))
    zero = jnp.zeros((), BF16)
    for h in range(q_sc.shape[0]):
        qh = q_ref[0, :, h * LANES:(h + 1) * LANES]
        q_sc[h, 0:tq, :] = jnp.where(lo_half, qh, zero)
        q_sc[h, tq:2 * tq, :] = jnp.where(lo_half, zero, qh)


def _diff_prompt_kernel(lam_ref, q_ref, k_ref, v_ref, g_ref, o_ref, q_sc, m_sc, l_sc, acc_sc,
                        *, tq, tk, out_scale):
    i = pl.program_id(1)
    n_heads = q_sc.shape[0]
    _stack_halves(q_ref, q_sc, tq)
    m_sc[...] = jnp.full(m_sc.shape, NEG_BIG, F32)
    l_sc[...] = jnp.zeros(l_sc.shape, F32)
    acc_sc[...] = jnp.zeros(acc_sc.shape, F32)

    def step(j, masked):
        start = pl.multiple_of(j * tk, tk)
        for h in range(n_heads):
            ls = slice(h * LANES, (h + 1) * LANES)
            kb = k_ref[0, pl.ds(start, tk), ls]
            vb = v_ref[0, pl.ds(start, tk), ls]
            s = _dot_t(q_sc[h], kb)
            if masked:
                row = lax.broadcasted_iota(jnp.int32, s.shape, 0) % tq
                col = lax.broadcasted_iota(jnp.int32, s.shape, 1)
                s = jnp.where(j * tk + col <= i * tq + row, s, NEG_BIG)
            m = m_sc[h]
            m_new = jnp.maximum(m, jnp.max(s, axis=-1, keepdims=True))
            alpha = jnp.exp(m - m_new)
            p = jnp.exp(s - m_new)
            l_sc[h] = alpha * l_sc[h] + jnp.sum(p, axis=-1, keepdims=True)
            acc_sc[h] = alpha * acc_sc[h] + _dot(p.astype(BF16), vb)
            m_sc[h] = m_new

    n_full = i * (tq // tk)

    @pl.loop(0, n_full)
    def _(j):
        step(j, False)

    for d in range(tq // tk):
        step(n_full + d, True)

    lam = lam_ref[0, 0]
    for h in range(n_heads):
        o_maps = acc_sc[h] / l_sc[h]
        o = o_maps[0:tq] - lam * o_maps[tq:2 * tq]
        ms = jnp.mean(o * o, axis=-1, keepdims=True)
        o_ref[0, :, h * LANES:(h + 1) * LANES] = (
            o * lax.rsqrt(ms + NORM_EPS) * g_ref[...] * out_scale).astype(o_ref.dtype)


def _diff_prompt_bounded_kernel(lam_ref, kmax_ref, q_ref, k_ref, v_ref, g_ref, o_ref, q_sc, c_sc, acc_sc,
                                *, tq, tk, out_scale):
    i = pl.program_id(1)
    n_heads = q_sc.shape[0]
    _stack_halves(q_ref, q_sc, tq)
    for h in range(n_heads):
        qf = q_sc[h].astype(F32)
        c_sc[h] = jnp.sqrt(jnp.sum(qf * qf, axis=-1, keepdims=True)) * kmax_ref[0, 0]
    acc_sc[...] = jnp.zeros(acc_sc.shape, F32)
    ones = jnp.ones((tk, LANES), BF16)

    def step(j, masked):
        start = pl.multiple_of(j * tk, tk)
        for h in range(n_heads):
            ls = slice(h * LANES, (h + 1) * LANES)
            kb = k_ref[0, pl.ds(start, tk), ls]
            vb = v_ref[0, pl.ds(start, tk), ls]
            s = _dot_t(q_sc[h], kb)
            if masked:
                row = lax.broadcasted_iota(jnp.int32, s.shape, 0) % tq
                col = lax.broadcasted_iota(jnp.int32, s.shape, 1)
                s = jnp.where(j * tk + col <= i * tq + row, s, NEG_BIG)
            p = jnp.exp(s - c_sc[h]).astype(BF16)
            acc_sc[h] += _dot(p, jnp.concatenate([vb, ones], axis=1))

    n_full = i * (tq // tk)

    @pl.loop(0, n_full)
    def _(j):
        step(j, False)

    for d in range(tq // tk):
        step(n_full + d, True)

    lam = lam_ref[0, 0]
    for h in range(n_heads):
        acc = acc_sc[h]
        o_maps = acc[:, 0:LANES] / acc[:, LANES:2 * LANES]
        o = o_maps[0:tq] - lam * o_maps[tq:2 * tq]
        ms = jnp.mean(o * o, axis=-1, keepdims=True)
        o_ref[0, :, h * LANES:(h + 1) * LANES] = (
            o * lax.rsqrt(ms + NORM_EPS) * g_ref[...] * out_scale).astype(o_ref.dtype)


def _diff_prompt_bounded(lam, kmax, q, k, v, g_subln, out_scale, tq, tk):
    b, s, w = q.shape
    n_heads = w // LANES
    kern = functools.partial(_diff_prompt_bounded_kernel, tq=tq, tk=tk, out_scale=out_scale)
    return pl.pallas_call(
        kern,
        grid=(b, s // tq),
        in_specs=[
            pl.BlockSpec(memory_space=pltpu.SMEM),
            pl.BlockSpec(memory_space=pltpu.SMEM),
            pl.BlockSpec((1, tq, w), lambda bi, i: (bi, i, 0)),
            pl.BlockSpec((1, s, w), lambda bi, i: (bi, 0, 0)),
            pl.BlockSpec((1, s, w), lambda bi, i: (bi, 0, 0)),
            pl.BlockSpec((1, LANES), lambda bi, i: (0, 0)),
        ],
        out_specs=pl.BlockSpec((1, tq, w), lambda bi, i: (bi, i, 0)),
        out_shape=jax.ShapeDtypeStruct((b, s, w), BF16),
        scratch_shapes=[
            pltpu.VMEM((n_heads, 2 * tq, LANES), BF16),
            pltpu.VMEM((n_heads, 2 * tq, 1), F32),
            pltpu.VMEM((n_heads, 2 * tq, 2 * LANES), F32),
        ],
        compiler_params=_cparams(("arbitrary", "arbitrary")),
        name="diff_prompt_bounded",
    )(lam, kmax, q, k, v, g_subln)


def _normed_head_bound(gain):
    return jnp.max(jnp.abs(gain.astype(F32))) * (DH ** 0.5) * BOUND_SLACK


def _diff_prompt(lam, q, k, v, g_subln, out_scale, tq, tk):
    b, s, w = q.shape
    n_heads = w // LANES
    kern = functools.partial(_diff_prompt_kernel, tq=tq, tk=tk, out_scale=out_scale)
    return pl.pallas_call(
        kern,
        grid=(b, s // tq),
        in_specs=[
            pl.BlockSpec(memory_space=pltpu.SMEM),
            pl.BlockSpec((1, tq, w), lambda bi, i: (bi, i, 0)),
            pl.BlockSpec((1, s, w), lambda bi, i: (bi, 0, 0)),
            pl.BlockSpec((1, s, w), lambda bi, i: (bi, 0, 0)),
            pl.BlockSpec((1, LANES), lambda bi, i: (0, 0)),
        ],
        out_specs=pl.BlockSpec((1, tq, w), lambda bi, i: (bi, i, 0)),
        out_shape=jax.ShapeDtypeStruct((b, s, w), BF16),
        scratch_shapes=[
            pltpu.VMEM((n_heads, 2 * tq, LANES), BF16),
            pltpu.VMEM((n_heads, 2 * tq, 1), F32),
            pltpu.VMEM((n_heads, 2 * tq, 1), F32),
            pltpu.VMEM((n_heads, 2 * tq, LANES), F32),
        ],
        compiler_params=_cparams(("arbitrary", "arbitrary")),
        name="diff_prompt",
    )(lam, q, k, v, g_subln)


def _sb_block(z, valid, tri_ref, r_in):
    log_1m = jnp.minimum(-z, 0.0) - jnp.log(1.0 + jnp.exp(-jnp.abs(z)))
    if valid is not None:
        log_1m = jnp.where(valid, log_1m, 0.0)
    later = _dot(jnp.concatenate(_split_bf16(log_1m), axis=1), tri_ref[...])
    a = jnp.exp((z + log_1m) + (later + r_in))
    if valid is not None:
        a = jnp.where(valid, a, 0.0)
    r_out = r_in + later[:, 0:1] + log_1m[:, 0:1]
    return a, r_out


def _head_rms(o, seg_ref, g):
    ms = _dot((o * o).astype(BF16), seg_ref[...])
    return o * lax.rsqrt(ms + NORM_EPS) * g


def _sb_prompt_kernel(q_ref, k_ref, v_ref, g_ref, tri_ref, seg_ref, o_ref, q_sc, r_sc, acc_sc, *, tq, tk):
    i = pl.program_id(1)
    n_blocks = q_sc.shape[0]
    _stack_halves(q_ref, q_sc, tq)
    r_sc[...] = jnp.zeros(r_sc.shape, F32)
    acc_sc[...] = jnp.zeros(acc_sc.shape, F32)

    def step(j, masked):
        start = pl.multiple_of(j * tk, tk)
        r_max = jnp.full((1, 1), NEG_BIG, F32)
        for p in range(n_blocks):
            ls = slice(p * LANES, (p + 1) * LANES)
            kb = k_ref[0, pl.ds(start, tk), ls]
            vb = v_ref[0, pl.ds(start, tk), ls]
            z = _dot_t(q_sc[p], kb)
            valid = None
            if masked:
                row = lax.broadcasted_iota(jnp.int32, z.shape, 0) % tq
                col = lax.broadcasted_iota(jnp.int32, z.shape, 1)
                valid = j * tk + col < i * tq + row
            a, r = _sb_block(z, valid, tri_ref, r_sc[p])
            acc_sc[p] += _dot(a.astype(BF16), vb)
            r_sc[p] = r
            r_max = jnp.maximum(r_max, jnp.max(r, axis=0, keepdims=True))
        return (r_max[0, 0] >= SB_DONE).astype(jnp.int32)

    n_full = i * (tq // tk)
    go = jnp.int32(1)
    for d in reversed(range(tq // tk)):
        go = step(n_full + d, True)

    def not_done(c):
        return jnp.logical_and(c[0] >= 0, c[1] > 0)

    def body(c):
        return c[0] - 1, step(c[0], False)

    lax.while_loop(not_done, body, (n_full - 1, go))
    lo_half = _half_masks((tq, LANES))
    for p in range(n_blocks):
        acc = acc_sc[p]
        o = jnp.where(lo_half, acc[0:tq], acc[tq:2 * tq])
        o_ref[0, :, p * LANES:(p + 1) * LANES] = _head_rms(o, seg_ref, g_ref[...]).astype(o_ref.dtype)


def _sb_prompt(q, k, v, g_sb, tri, seg, tq, tk):
    b, s, w = q.shape
    n_blocks = w // LANES
    kern = functools.partial(_sb_prompt_kernel, tq=tq, tk=tk)
    return pl.pallas_call(
        kern,
        grid=(b, s // tq),
        in_specs=[
            pl.BlockSpec((1, tq, w), lambda bi, i: (bi, i, 0)),
            pl.BlockSpec((1, s, w), lambda bi, i: (bi, 0, 0)),
            pl.BlockSpec((1, s, w), lambda bi, i: (bi, 0, 0)),
            pl.BlockSpec((1, LANES), lambda bi, i: (0, 0)),
            pl.BlockSpec(tri.shape, lambda bi, i: (0, 0)),
            pl.BlockSpec(seg.shape, lambda bi, i: (0, 0)),
        ],
        out_specs=pl.BlockSpec((1, tq, w), lambda bi, i: (bi, i, 0)),
        out_shape=jax.ShapeDtypeStruct((b, s, w), BF16),
        scratch_shapes=[
            pltpu.VMEM((n_blocks, 2 * tq, LANES), BF16),
            pltpu.VMEM((n_blocks, 2 * tq, 1), F32),
            pltpu.VMEM((n_blocks, 2 * tq, LANES), F32),
        ],
        compiler_params=_cparams(("arbitrary", "arbitrary")),
        name="sb_prompt",
    )(q, k, v, g_sb, tri, seg)


Q_ROWS = 16


def _page_spec(n_pages, first_of_chunk, r, page_shape):
    def index_map(b, c, pt_ref):
        return (pt_ref[b * n_pages + first_of_chunk(c) + r], 0, 0)
    return pl.BlockSpec((1,) + page_shape, index_map)


def _page_scores(q, k_refs, rows):
    return _dot(q, jnp.concatenate([r[0, rows, :] for r in k_refs], axis=1).astype(BF16))


def _diff_sample_kernel(pt_ref, lam_ref, q_ref, kn_ref, vn_ref, g_ref, *rest, n_pp, t_new, out_scale):
    k_refs, v_refs = rest[:n_pp], rest[n_pp:2 * n_pp]
    o_ref, m_sc, l_sc, acc_sc = rest[2 * n_pp:]
    c = pl.program_id(1)
    n_heads = q_ref.shape[-1] // LANES

    @pl.when(c == 0)
    def _():
        m_sc[...] = jnp.full(m_sc.shape, NEG_BIG, F32)
        l_sc[...] = jnp.zeros(l_sc.shape, F32)
        acc_sc[...] = jnp.zeros(acc_sc.shape, F32)

    def update(h, s, pv):
        m = m_sc[h]
        m_new = jnp.maximum(m, jnp.max(s, axis=-1, keepdims=True))
        alpha = jnp.exp(m - m_new)
        p = jnp.exp(s - m_new)
        l_sc[h] = alpha * l_sc[h] + jnp.sum(p, axis=-1, keepdims=True)
        acc_sc[h] = alpha * acc_sc[h] + pv(p.astype(BF16))
        m_sc[h] = m_new

    page_size = k_refs[0].shape[-1]
    heads = range(n_heads)
    lanes = [slice(h * LANES, (h + 1) * LANES) for h in heads]
    m_old = [m_sc[h] for h in heads]
    l_old = [l_sc[h] for h in heads]
    acc_old = [acc_sc[h] for h in heads]
    s = [_page_scores(q_ref[0, :, lanes[h]], k_refs, lanes[h]) for h in heads]
    m_new = [jnp.maximum(m_old[h], jnp.max(s[h], axis=-1, keepdims=True)) for h in heads]
    p = [jnp.exp(s[h] - m_new[h]) for h in heads]
    alpha = [jnp.exp(m_old[h] - m_new[h]) for h in heads]
    for h in heads:
        vb = jnp.concatenate([v_r[0, pl.ds(h, page_size, stride=n_heads), :] for v_r in v_refs], axis=0)
        pv = _dot(p[h].astype(BF16), vb.astype(BF16))
        m_sc[h] = m_new[h]
        l_sc[h] = alpha[h] * l_old[h] + jnp.sum(p[h], axis=-1, keepdims=True)
        acc_sc[h] = alpha[h] * acc_old[h] + pv

    @pl.when(c == pl.num_programs(1) - 1)
    def _():
        lam = lam_ref[0, 0]
        row = lax.broadcasted_iota(jnp.int32, (Q_ROWS, Q_ROWS), 0)
        col = lax.broadcasted_iota(jnp.int32, (Q_ROWS, Q_ROWS), 1)
        valid = jnp.logical_and(col < t_new, col <= row % t_new)
        for h in range(n_heads):
            ls = slice(h * LANES, (h + 1) * LANES)
            s = jnp.where(valid, _dot_t(q_ref[0, :, ls], kn_ref[0, :, ls]), NEG_BIG)
            update(h, s, lambda p, ls=ls: _dot(p, vn_ref[0, :, ls]))
            o_maps = acc_sc[h] / l_sc[h]
            o = o_maps[0:8] - lam * o_maps[t_new:t_new + 8]
            ms = jnp.mean(o * o, axis=-1, keepdims=True)
            o_ref[0, :, ls] = o * lax.rsqrt(ms + NORM_EPS) * g_ref[...] * out_scale


def _diff_sample(pt_flat, lam, q16, k_new, v_new, g_subln, cache_k, cache_v, n_pages, n_pp, t_new, out_scale):
    nb, _, w = q16.shape
    n_heads = w // LANES
    per_seq = lambda b, c, pt: (b, 0, 0)
    chunk0 = lambda c: c * n_pp
    pages = lambda cache: [_page_spec(n_pages, chunk0, r, cache.shape[1:]) for r in range(n_pp)]
    kern = functools.partial(_diff_sample_kernel, n_pp=n_pp, t_new=t_new, out_scale=out_scale)
    grid_spec = pltpu.PrefetchScalarGridSpec(
        num_scalar_prefetch=1,
        grid=(nb, n_pages // n_pp),
        in_specs=[
            pl.BlockSpec(memory_space=pltpu.SMEM),
            pl.BlockSpec((1, Q_ROWS, w), per_seq),
            pl.BlockSpec((1, Q_ROWS, w), per_seq),
            pl.BlockSpec((1, Q_ROWS, w), per_seq),
            pl.BlockSpec((1, LANES), lambda b, c, pt: (0, 0)),
        ] + pages(cache_k) + pages(cache_v),
        out_specs=pl.BlockSpec((1, 8, w), per_seq),
        scratch_shapes=[
            pltpu.VMEM((n_heads, Q_ROWS, 1), F32),
            pltpu.VMEM((n_heads, Q_ROWS, 1), F32),
            pltpu.VMEM((n_heads, Q_ROWS, LANES), F32),
        ],
    )
    return pl.pallas_call(
        kern,
        grid_spec=grid_spec,
        out_shape=jax.ShapeDtypeStruct((nb, 8, w), F32),
        compiler_params=_cparams(("arbitrary", "arbitrary")),
        name="diff_sample",
    )(pt_flat, lam, q16, k_new, v_new, g_subln, *([cache_k] * n_pp), *([cache_v] * n_pp))


def _sb_sample_kernel(pt_ref, q_ref, kn_ref, vn_ref, g_ref, tri_ref, trin_ref, seg_ref, acc_in, r_in, *rest,
                      n_pp, t_new, with_new):
    k_refs, v_refs = rest[:n_pp], rest[n_pp:2 * n_pp]
    o_ref, acc_ref, r_ref = rest[2 * n_pp:]
    c = pl.program_id(1)
    n_blocks = q_ref.shape[-1] // LANES

    @pl.when(c == 0)
    def _():
        if with_new:
            row = lax.broadcasted_iota(jnp.int32, (Q_ROWS, Q_ROWS), 0)
            col = lax.broadcasted_iota(jnp.int32, (Q_ROWS, Q_ROWS), 1)
            valid = jnp.logical_and(col < t_new, col < row % t_new)
            for p in range(n_blocks):
                ls = slice(p * LANES, (p + 1) * LANES)
                z = _dot_t(q_ref[0, :, ls], kn_ref[0, :, ls])
                a, r = _sb_block(z, valid, trin_ref, jnp.zeros((Q_ROWS, 1), F32))
                acc_ref[0, :, ls] = _dot(a.astype(BF16), vn_ref[0, :, ls])
                r_ref[0, :, ls] = jnp.broadcast_to(r, (Q_ROWS, LANES))
        else:
            acc_ref[...] = acc_in[...]
            r_ref[...] = r_in[...]

    blocks = range(n_blocks)
    lanes = [slice(p * LANES, (p + 1) * LANES) for p in blocks]
    r_old = [r_ref[0, :, p * LANES:p * LANES + 1] for p in blocks]
    acc_old = [acc_ref[0, :, lanes[p]] for p in blocks]
    z = [_page_scores(q_ref[0, :, lanes[p]], k_refs, lanes[p]) for p in blocks]
    ar = [_sb_block(z[p], None, tri_ref, r_old[p]) for p in blocks]
    for p in blocks:
        vt = jnp.concatenate([v_r[0, lanes[p], :] for v_r in v_refs], axis=1).astype(BF16)
        acc_ref[0, :, lanes[p]] = acc_old[p] + _dot_t(ar[p][0].astype(BF16), vt)
        r_ref[0, :, lanes[p]] = jnp.broadcast_to(ar[p][1], (Q_ROWS, LANES))

    @pl.when(c == pl.num_programs(1) - 1)
    def _():
        lo_half = _half_masks((8, LANES))
        for p in range(n_blocks):
            ls = slice(p * LANES, (p + 1) * LANES)
            acc = acc_ref[0, :, ls]
            o = jnp.where(lo_half, acc[0:8], acc[t_new:t_new + 8])
            o_ref[0, :, ls] = _head_rms(o, seg_ref, g_ref[...])


def _sb_sample(pt_flat, q16, k_new, v_new, g_sb, tri, tri_new, seg, acc_in, r_in, cache_k, cache_v,
               n_pages, page_hi, n_chunks, n_pp, t_new, with_new):
    nb, _, w = q16.shape
    per_seq = lambda b, c, pt: (b, 0, 0)
    fixed = lambda b, c, pt: (0, 0)
    first = lambda c: page_hi - (c + 1) * n_pp
    pages = lambda cache: [_page_spec(n_pages, first, r, cache.shape[1:]) for r in range(n_pp)]
    kern = functools.partial(_sb_sample_kernel, n_pp=n_pp, t_new=t_new, with_new=with_new)
    state_spec = pl.BlockSpec((1, Q_ROWS, w), per_seq)
    grid_spec = pltpu.PrefetchScalarGridSpec(
        num_scalar_prefetch=1,
        grid=(nb, n_chunks),
        in_specs=[
            state_spec, state_spec, state_spec,
            pl.BlockSpec((1, LANES), fixed),
            pl.BlockSpec(tri.shape, fixed),
            pl.BlockSpec(tri_new.shape, fixed),
            pl.BlockSpec(seg.shape, fixed),
            state_spec, state_spec,
        ] + pages(cache_k) + pages(cache_v),
        out_specs=[pl.BlockSpec((1, 8, w), per_seq), state_spec, state_spec],
    )
    state = jax.ShapeDtypeStruct((nb, Q_ROWS, w), F32)
    return pl.pallas_call(
        kern,
        grid_spec=grid_spec,
        out_shape=[jax.ShapeDtypeStruct((nb, 8, w), F32), state, state],
        compiler_params=_cparams(("arbitrary", "arbitrary")),
        name="sb_sample_new" if with_new else "sb_sample_rest",
    )(pt_flat, q16, k_new, v_new, g_sb, tri, tri_new, seg, acc_in, r_in,
      *([cache_k] * n_pp), *([cache_v] * n_pp))


def _pack_bf16_pairs(x):
    c = x.shape[-1] // 2
    bits = lax.bitcast_convert_type(x.astype(F32), jnp.uint32)
    return (bits[:, :c] >> 16) | (bits[:, c:] & jnp.uint32(0xFFFF0000))


def _unpack_bf16_pairs(w):
    lo = lax.bitcast_convert_type(w << 16, F32)
    hi = lax.bitcast_convert_type(w & jnp.uint32(0xFFFF0000), F32)
    return jnp.concatenate([lo, hi], axis=1).astype(BF16)


def _finish_kernel(x_ref, od_ref, os_ref, wo_ref, g_ref, wr_hi_ref, wr_lo_ref, br_ref,
                   h_ref, hn_ref, route_ref):
    half = od_ref.shape[-1]
    h = x_ref[...] + _dot(od_ref[...], wo_ref[0:half, :]) + _dot(os_ref[...], wo_ref[half:, :])
    h_ref[...] = h
    ms = jnp.mean(h * h, axis=-1, keepdims=True)
    hn = h * lax.rsqrt(ms + NORM_EPS) * g_ref[...]
    hn_hi, hn_lo = _split_bf16(hn)
    hn_ref[...] = _pack_bf16_pairs(hn_hi)

    logits = (_dot(hn_hi, wr_hi_ref[...]) + _dot(hn_lo, wr_hi_ref[...]) + _dot(hn_hi, wr_lo_ref[...])
              + br_ref[...])
    n_exp = N_GROUPS * EXPERTS_PER_GROUP
    lane = lax.broadcasted_iota(jnp.int32, logits.shape, 1)

    def first_argmax(v, vmax):
        return jnp.min(jnp.where(v == vmax, lane, LANES), axis=-1, keepdims=True)

    lg = jnp.where(jnp.logical_and(lane >= n_exp, lane < n_exp + N_GROUPS), logits, NEG_BIG)
    mg = jnp.max(lg, axis=-1, keepdims=True)
    grp = first_argmax(lg, mg) - n_exp
    p_top = 1.0 / jnp.sum(jnp.exp(lg - mg), axis=-1, keepdims=True)

    le = jnp.where(jnp.logical_and(lane < n_exp, lane // EXPERTS_PER_GROUP == grp), logits, NEG_BIG)
    m1 = jnp.max(le, axis=-1, keepdims=True)
    i1 = first_argmax(le, m1)
    le2 = jnp.where(lane == i1, NEG_BIG, le)
    m2 = jnp.max(le2, axis=-1, keepdims=True)
    i2 = first_argmax(le2, m2)
    e2 = jnp.exp(m2 - m1)
    gate1 = p_top / (1.0 + e2)
    gate2 = p_top * e2 / (1.0 + e2)
    route = jnp.where(lane == 0, i1.astype(F32),
                      jnp.where(lane == 1, i2.astype(F32),
                                jnp.where(lane == 2, gate1, jnp.where(lane == 3, gate2, 0.0))))
    route_ref[...] = route


def _finish(x2d, od, osb, w_o_bf, g_ffn, wr_hi, wr_lo, b_r, tm):
    n, d = x2d.shape
    half = od.shape[-1]
    row = lambda i: (i, 0)
    fixed = lambda i: (0, 0)
    return pl.pallas_call(
        _finish_kernel,
        grid=(n // tm,),
        in_specs=[
            pl.BlockSpec((tm, d), row),
            pl.BlockSpec((tm, half), row),
            pl.BlockSpec((tm, half), row),
            pl.BlockSpec(w_o_bf.shape, fixed),
            pl.BlockSpec((1, d), fixed),
            pl.BlockSpec(wr_hi.shape, fixed),
            pl.BlockSpec(wr_lo.shape, fixed),
            pl.BlockSpec((1, LANES), fixed),
        ],
        out_specs=[pl.BlockSpec((tm, d), row), pl.BlockSpec((tm, d // 2), row), pl.BlockSpec((tm, LANES), row)],
        out_shape=[jax.ShapeDtypeStruct((n, d), F32), jax.ShapeDtypeStruct((n, d // 2), jnp.uint32),
                   jax.ShapeDtypeStruct((n, LANES), F32)],
        compiler_params=_cparams(("arbitrary",)),
        name="out_proj_router",
    )(x2d, od, osb, w_o_bf, g_ffn, wr_hi, wr_lo, b_r)


def _expert_kernel(be_ref, nact_ref, x_ref, wg_ref, wu_ref, wd_ref, y_ref):
    blk = pl.program_id(0)

    @pl.when(blk < nact_ref[0])
    def _():
        x = _unpack_bf16_pairs(x_ref[...])
        gate = _dot(x, wg_ref[0].astype(BF16))
        up = _dot(x, wu_ref[0].astype(BF16))
        hidden = (gate * jax.nn.sigmoid(gate) * up).astype(BF16)
        y_ref[...] = _dot(hidden, wd_ref[0].astype(BF16))

    @pl.when(blk >= nact_ref[0])
    def _():
        y_ref[...] = jnp.zeros(y_ref.shape, y_ref.dtype)


def _expert_ffn(block_exp, n_active, xs, w_gate, w_up, w_down, bm):
    n_slots = xs.shape[0]
    d, ff = w_gate.shape[-2:]
    grid_spec = pltpu.PrefetchScalarGridSpec(
        num_scalar_prefetch=2,
        grid=(n_slots // bm,),
        in_specs=[
            pl.BlockSpec((bm, xs.shape[1]), lambda i, be, na: (i, 0)),
            pl.BlockSpec((1, d, ff), lambda i, be, na: (be[i], 0, 0)),
            pl.BlockSpec((1, d, ff), lambda i, be, na: (be[i], 0, 0)),
            pl.BlockSpec((1, ff, d), lambda i, be, na: (be[i], 0, 0)),
        ],
        out_specs=pl.BlockSpec((bm, d), lambda i, be, na: (i, 0)),
    )
    return pl.pallas_call(
        _expert_kernel,
        grid_spec=grid_spec,
        out_shape=jax.ShapeDtypeStruct((n_slots, d), F32),
        compiler_params=_cparams(("arbitrary",)),
        name="expert_ffn",
    )(block_exp, n_active, xs, w_gate, w_up, w_down)


def _moe(h_parts, hn_packed, route, w_gate, w_up, w_down, bm):
    n_tok = hn_packed.shape[0]
    n_exp = w_gate.shape[0]
    ids = route[:, 0:2].astype(jnp.int32)
    gates = route[:, 2:4]
    n_assign = 2 * n_tok
    flat_e = ids.reshape(-1)
    experts = jnp.arange(n_exp, dtype=jnp.int32)
    order = jnp.argsort(flat_e, stable=True).astype(jnp.int32)
    rank = jnp.argsort(order).astype(jnp.int32)
    counts = jnp.sum((flat_e[:, None] == experts[None, :]).astype(jnp.int32), axis=0)
    padded = (counts + bm - 1) // bm * bm
    start = jnp.cumsum(counts) - counts
    pend = jnp.cumsum(padded)
    pstart = pend - padded
    dest = (pstart[flat_e] + rank - start[flat_e]).reshape(n_tok, 2)
    n_blocks = -(-n_assign // bm) + n_exp
    blk_start = jnp.arange(n_blocks, dtype=jnp.int32) * bm
    block_exp = jnp.minimum(jnp.sum((blk_start[:, None] >= pend[None, :]).astype(jnp.int32), axis=1), n_exp - 1)
    n_active = (pend[-1:] // bm).astype(jnp.int32)
    slot_exp = jnp.repeat(block_exp, bm)
    offset = jnp.arange(n_blocks * bm, dtype=jnp.int32) - pstart[slot_exp]
    src = jnp.minimum(start[slot_exp] + offset, n_assign - 1)
    slot_tok = jnp.where(offset < counts[slot_exp], order[src] // 2, 0)
    xs = hn_packed[slot_tok]
    yb = _expert_ffn(block_exp, n_active, xs, w_gate, w_up, w_down, bm)
    outs, lo = [], 0
    for h in h_parts:
        hi = lo + h.shape[0]
        outs.append(h + (gates[lo:hi, 0:1] * yb[dest[lo:hi, 0]] + gates[lo:hi, 1:2] * yb[dest[lo:hi, 1]]))
        lo = hi
    return outs


def _block_diag_mean(width, group):
    idx = jnp.arange(width) // group
    return jnp.where(idx[:, None] == idx[None, :], 1.0 / group, 0.0).astype(BF16)


def _tri(n):
    idx = jnp.arange(n)
    tri = (idx[:, None] > idx[None, :]).astype(BF16)
    return jnp.concatenate([tri, tri], axis=0)


def _sample_rows(a, nb, t):
    w = a.shape[-1]
    a = a.reshape(nb, t, w)
    lo = (jnp.arange(w) % LANES) < DH
    zeros = jnp.zeros((nb, Q_ROWS - 2 * t, w), a.dtype)
    return jnp.concatenate([jnp.where(lo, a, 0), jnp.where(lo, 0, a), zeros], axis=1)


def _pad_rows(a, nb, t):
    w = a.shape[-1]
    a = a.reshape(nb, t, w)
    return jnp.concatenate([a, jnp.zeros((nb, Q_ROWS - t, w), a.dtype)], axis=1)


def kernel(x_prompt, x_sample, cache_k_diff, cache_v_diff, cache_k_sb, cache_v_sb, page_table, g_attn_norm, w_in, g_q_norm, g_k_norm, lambda_q1, lambda_k1, lambda_q2, lambda_k2, g_subln, g_sb_out, w_o, g_ffn_norm, w_router_group, b_router_group, w_router_expert, b_router_expert, w_exp_gate, w_exp_up, w_exp_down):
    depth = w_in.shape[0]
    b, s, d = x_prompt.shape
    nb, t_new, _ = x_sample.shape
    n_phys, page_size = cache_k_diff.shape[1], cache_k_diff.shape[2]
    n_pages = page_table.shape[1]
    past_len = n_pages * page_size
    width = w_in.shape[-1] // 6
    n_exp = w_exp_gate.shape[1]

    tm = min(256, nb * t_new)
    tq, tk = min(PROMPT_TQ, s), min(PROMPT_TK, s)
    n_pp_diff = min(16, n_pages)
    n_pp_sb = min(2, n_pages)
    bm = 256

    seg64 = _block_diag_mean(width, DH)
    seg64_lane = _block_diag_mean(LANES, DH)
    tri_p = _tri(tk)
    tri_s = _tri(n_pp_sb * page_size)
    tri_new = _tri(Q_ROWS)
    pos_p = _rope_tables(jnp.arange(s), 1)
    pos_s = _rope_tables(past_len + jnp.arange(t_new), tm // t_new)

    hp = x_prompt.reshape(b * s, d)
    hs = x_sample.reshape(nb * t_new, d)
    outs = [[] for _ in range(8)]
    for layer in range(depth):
        lam_init = 0.8 - 0.6 * math.exp(-0.3 * layer)
        lam = (jnp.exp(jnp.sum(lambda_q1[layer].astype(F32) * lambda_k1[layer].astype(F32)))
               - jnp.exp(jnp.sum(lambda_q2[layer].astype(F32) * lambda_k2[layer].astype(F32))) + lam_init)
        lam = lam.reshape(1, 1).astype(F32)
        out_scale = 1.0 - lam_init
        g_attn = g_attn_norm[layer].reshape(1, d)
        w_in_bf = w_in[layer].astype(BF16)
        gq = jnp.tile(g_q_norm[layer], width // DH).reshape(1, width)
        gk = jnp.tile(g_k_norm[layer], width // DH).reshape(1, width)
        g_sub = g_subln[layer].reshape(1, LANES)
        g_sb = jnp.tile(g_sb_out[layer], LANES // DH).reshape(1, LANES)

        (kd_p, vd_p, ks_p, vs_p, qd_pb, kd_pb, vd_pb, qs_pb, ks_pb, vs_pb) = _project(
            hp, pos_p, g_attn, w_in_bf, gq, gk, seg64, min(256, s))
        (kd_s, vd_s, ks_s, vs_s, qd_sb, kd_sb, vd_sb, qs_sb, ks_sb, vs_sb) = _project(
            hs, pos_s, g_attn, w_in_bf, gq, gk, seg64, tm)

        r3 = lambda a: a.reshape(b, s, width)
        qd3, kd3, vd3 = r3(qd_pb), r3(kd_pb), r3(vd_pb)
        qmax = _normed_head_bound(g_q_norm[layer]) * (DH ** -0.5)
        kmax = _normed_head_bound(g_k_norm[layer])
        od_p = lax.cond(
            qmax * kmax <= BOUND_MAX,
            lambda: _diff_prompt_bounded(lam, kmax.reshape(1, 1), qd3, kd3, vd3, g_sub, out_scale, tq, tk),
            lambda: _diff_prompt(lam, qd3, kd3, vd3, g_sub, out_scale, tq, tk))
        osb_p = _sb_prompt(r3(qs_pb), r3(ks_pb), r3(vs_pb), g_sb, tri_p, seg64_lane, tq, tk)

        pt_flat = (page_table.astype(jnp.int32) + layer * n_phys).reshape(-1)
        transposed_pages = lambda c: jnp.transpose(c, (0, 1, 3, 4, 2)).reshape(depth * n_phys, width, page_size)
        v_diff_pages = cache_v_diff.reshape(depth * n_phys, page_size * (width // DV_DIFF_W), DV_DIFF_W)
        od_s = _diff_sample(pt_flat, lam, _sample_rows(qd_sb, nb, t_new), _pad_rows(kd_sb, nb, t_new),
                            _pad_rows(vd_sb, nb, t_new), g_sub, transposed_pages(cache_k_diff),
                            v_diff_pages, n_pages, n_pp_diff, t_new, out_scale)
        q16s = _sample_rows(qs_sb, nb, t_new)
        kn16, vn16 = _pad_rows(ks_sb, nb, t_new), _pad_rows(vs_sb, nb, t_new)
        cks, cvs = transposed_pages(cache_k_sb), transposed_pages(cache_v_sb)
        zeros_state = jnp.zeros((nb, Q_ROWS, width), F32)
        sb_args = (pt_flat, q16s, kn16, vn16, g_sb, tri_s, tri_new, seg64_lane)
        osb_s, acc_s, r_s = _sb_sample(*sb_args, zeros_state, zeros_state, cks, cvs,
                                       n_pages, n_pages, 1, n_pp_sb, t_new, True)
        n_rest = (n_pages - n_pp_sb) // n_pp_sb
        if n_rest > 0:
            unfinished = jnp.max(r_s[:, :2 * t_new, :]) >= SB_DONE
            osb_s = lax.cond(
                unfinished,
                lambda: _sb_sample(*sb_args, acc_s, r_s, cks, cvs, n_pages, n_pages - n_pp_sb,
                                   n_rest, n_pp_sb, t_new, False)[0],
                lambda: osb_s)

        w_o_bf = w_o[layer].astype(BF16)
        g_ffn = g_ffn_norm[layer].reshape(1, d)
        w_r = jnp.zeros((d, LANES), F32)
        w_r = w_r.at[:, :n_exp].set(w_router_expert[layer]).at[:, n_exp:n_exp + N_GROUPS].set(w_router_group[layer])
        wr_hi, wr_lo = _split_bf16(w_r)
        b_r = jnp.zeros((1, LANES), F32)
        b_r = b_r.at[0, :n_exp].set(b_router_expert[layer]).at[0, n_exp:n_exp + N_GROUPS].set(b_router_group[layer])

        od_s2 = od_s[:, :t_new].reshape(nb * t_new, width).astype(BF16)
        osb_s2 = osb_s[:, :t_new].reshape(nb * t_new, width).astype(BF16)
        fin = lambda xx, od, osb, tile: _finish(xx, od, osb, w_o_bf, g_ffn, wr_hi, wr_lo, b_r, tile)
        h_p, hn_p, route_p = fin(hp, od_p.reshape(b * s, width), osb_p.reshape(b * s, width), min(256, s))
        h_s, hn_s, route_s = fin(hs, od_s2, osb_s2, tm)

        hp, hs = _moe((h_p, h_s), jnp.concatenate([hn_p, hn_s]), jnp.concatenate([route_p, route_s]),
                      w_exp_gate[layer], w_exp_up[layer], w_exp_down[layer], bm)

        for lst, val in zip(outs, (kd_p, vd_p, ks_p, vs_p, kd_s, vd_s, ks_s, vs_s)):
            lst.append(val)

    h_diff, h_sb = width // DV_DIFF_W, width // DH
    shapes_p = [(b, s, 2 * h_diff, DH), (b, s, h_diff, DV_DIFF_W), (b, s, h_sb, DH), (b, s, h_sb, DH)]
    shapes_s = [(nb, t_new) + sh[2:] for sh in shapes_p]
    stacked = [jnp.stack([v.reshape(sh) for v in lst]) for lst, sh in zip(outs, shapes_p + shapes_s)]
    return (hp.reshape(b, s, d), hs.reshape(nb, t_new, d), *stacked)
```

```python
import functools
import math

import jax
import jax.numpy as jnp
from jax import lax
from jax.experimental import pallas as pl
from jax.experimental.pallas import tpu as pltpu

F32 = jnp.float32
BF16 = jnp.bfloat16

NORM_EPS = 1e-6
ROPE_THETA = 10000.0
DH = 64
DV_DIFF_W = 2 * DH
LANES = 128
N_GROUPS = 4
EXPERTS_PER_GROUP = 8
NEG_BIG = -1e30
SB_DONE = -110.0
VMEM_LIMIT = 56 * 1024 * 1024
BOUND_MAX = 40.0
BOUND_SLACK = 1.01
PROMPT_TQ = 256
PROMPT_TK = 256


def _cparams(sem):
    return pltpu.CompilerParams(dimension_semantics=sem, vmem_limit_bytes=VMEM_LIMIT)


def _dot_t(a, b):
    return lax.dot_general(a, b, (((1,), (1,)), ((), ())), preferred_element_type=F32)


def _dot(a, b):
    return jnp.dot(a, b, preferred_element_type=F32)


def _split_bf16(x):
    hi = x.astype(BF16)
    lo = (x - hi.astype(F32)).astype(BF16)
    return hi, lo


def _proj_kernel(x_ref, g_ref, w_ref, gq_ref, gk_ref, cos_ref, sin_ref, seg_ref,
                 kd_ref, vd_ref, ks_ref, vs_ref,
                 qd_bf, kd_bf, vd_bf, qs_bf, ks_bf, vs_bf, *, transposed):
    x = x_ref[...]
    ms = jnp.mean(x * x, axis=-1, keepdims=True)
    xn = (x * lax.rsqrt(ms + NORM_EPS) * g_ref[...]).astype(BF16)
    width = vd_ref.shape[-1]

    def put_key_like(f32_ref, bf_ref, val):
        if transposed:
            val_t = val.T
            f32_ref[0] = val_t
            if bf_ref is not None:
                bf_ref[0, 0] = val_t.astype(BF16)
        else:
            f32_ref[...] = val
            if bf_ref is not None:
                bf_ref[...] = val.astype(BF16)

    def seg(c):
        return _dot(xn, w_ref[:, c * width:(c + 1) * width])

    reps = width // LANES
    cos = jnp.concatenate([cos_ref[...]] * reps, axis=1)
    sin = jnp.concatenate([sin_ref[...]] * reps, axis=1)
    lane = lax.broadcasted_iota(jnp.int32, (x.shape[0], width), 1)
    first_half = (lane % DH) < (DH // 2)

    def norm_rope(p, g):
        ms_h = _dot((p * p).astype(BF16), seg_ref[...])
        y = p * lax.rsqrt(ms_h + NORM_EPS) * g
        partner = jnp.where(first_half, pltpu.roll(y, width - DH // 2, 1), pltpu.roll(y, DH // 2, 1))
        return y * cos + partner * sin

    scale = DH ** -0.5
    qd = norm_rope(seg(0), gq_ref[...])
    qd_bf[...] = (qd * scale).astype(BF16)
    put_key_like(kd_ref, kd_bf, norm_rope(seg(1), gk_ref[...]))
    vd = seg(2)
    vd_ref[...] = vd
    vd_bf[...] = vd.astype(BF16)
    qs_bf[...] = (seg(3) * scale).astype(BF16)
    put_key_like(ks_ref, ks_bf, seg(4))
    vs = seg(5)
    put_key_like(vs_ref, None, vs)
    vs_bf[...] = vs.astype(BF16)


def _rope_tables(pos, reps):
    half = DH // 2
    inv = jnp.power(jnp.float32(ROPE_THETA), -(jnp.arange(half, dtype=F32) * 2.0 / DH))
    ang = pos.astype(F32)[:, None] * inv[None, :]
    cos, sin = jnp.cos(ang), jnp.sin(ang)
    cos_t = jnp.concatenate([cos, cos] * (LANES // DH), axis=1)
    sin_t = jnp.concatenate([-sin, sin] * (LANES // DH), axis=1)
    return jnp.tile(cos_t, (reps, 1)), jnp.tile(sin_t, (reps, 1))


def _project(x2d, pos_tables, g_attn, w_in_bf, gq, gk, seg_mat, tm, seq_len=None):
    n, d = x2d.shape
    width = w_in_bf.shape[1] // 6
    cos_t, sin_t = pos_tables
    n_pos_blocks = cos_t.shape[0] // tm
    row = lambda i: (i, 0)
    fixed = lambda i: (0, 0)
    out_f32 = jax.ShapeDtypeStruct((n, width), F32)
    out_bf = jax.ShapeDtypeStruct((n, width), BF16)
    natural = pl.BlockSpec((tm, width), row)
    out_specs = [natural] * 10
    out_shape = [out_f32] * 4 + [out_bf] * 6
    if seq_len is not None:
        n_sb = seq_len // tm
        key_f32 = jax.ShapeDtypeStruct((n // seq_len, width, seq_len), F32)
        key_bf = jax.ShapeDtypeStruct((n // seq_len, n_sb, width, tm), BF16)
        spec_f32 = pl.BlockSpec((1, width, tm), lambda i: (i // n_sb, 0, i % n_sb))
        spec_bf = pl.BlockSpec((1, 1, width, tm), lambda i: (i // n_sb, i % n_sb, 0, 0))
        out_specs = [spec_f32, natural, spec_f32, spec_f32, natural, spec_bf, natural, natural, spec_bf, natural]
        out_shape = [key_f32, out_f32, key_f32, key_f32, out_bf, key_bf, out_bf, out_bf, key_bf, out_bf]
    return pl.pallas_call(
        functools.partial(_proj_kernel, transposed=seq_len is not None),
        grid=(n // tm,),
        in_specs=[
            pl.BlockSpec((tm, d), row),
            pl.BlockSpec((1, d), fixed),
            pl.BlockSpec(w_in_bf.shape, fixed),
            pl.BlockSpec((1, width), fixed),
            pl.BlockSpec((1, width), fixed),
            pl.BlockSpec((tm, LANES), lambda i: (i % n_pos_blocks, 0)),
            pl.BlockSpec((tm, LANES), lambda i: (i % n_pos_blocks, 0)),
            pl.BlockSpec(seg_mat.shape, fixed),
        ],
        out_specs=out_specs,
        out_shape=out_shape,
        compiler_params=_cparams(("arbitrary",)),
        name="in_proj",
    )(x2d, g_attn, w_in_bf, gq, gk, cos_t, sin_t, seg_mat)


def _half_masks(shape):
    lane = lax.broadcasted_iota(jnp.int32, shape, 1)
    return lane < DH


def _stack_halves(q_ref, q_sc, tq):
    lo_half = _half_masks((tq, LANES))
    zero = jnp.zeros((), BF16)
    for h in range(q_sc.shape[0]):
        qh = q_ref[0, :, h * LANES:(h + 1) * LANES]
        q_sc[h, 0:tq, :] = jnp.where(lo_half, qh, zero)
        q_sc[h, tq:2 * tq, :] = jnp.where(lo_half, zero, qh)


def _diff_prompt_kernel(lam_ref, q_ref, k_ref, v_ref, g_ref, o_ref, q_sc, m_sc, l_sc, acc_sc,
                        *, tq, tk, out_scale):
    i = pl.program_id(1)
    n_heads = q_sc.shape[0]
    _stack_halves(q_ref, q_sc, tq)
    m_sc[...] = jnp.full(m_sc.shape, NEG_BIG, F32)
    l_sc[...] = jnp.zeros(l_sc.shape, F32)
    acc_sc[...] = jnp.zeros(acc_sc.shape, F32)

    def step(j, masked):
        start = pl.multiple_of(j * tk, tk)
        for h in range(n_heads):
            ls = slice(h * LANES, (h + 1) * LANES)
            kb = k_ref[0, j, ls, :]
            vb = v_ref[0, pl.ds(start, tk), ls]
            s = _dot(q_sc[h], kb)
            if masked:
                row = lax.broadcasted_iota(jnp.int32, s.shape, 0) % tq
                col = lax.broadcasted_iota(jnp.int32, s.shape, 1)
                s = jnp.where(j * tk + col <= i * tq + row, s, NEG_BIG)
            m = m_sc[h]
            m_new = jnp.maximum(m, jnp.max(s, axis=-1, keepdims=True))
            alpha = jnp.exp(m - m_new)
            p = jnp.exp(s - m_new)
            l_sc[h] = alpha * l_sc[h] + jnp.sum(p, axis=-1, keepdims=True)
            acc_sc[h] = alpha * acc_sc[h] + _dot(p.astype(BF16), vb)
            m_sc[h] = m_new

    n_full = i * (tq // tk)

    @pl.loop(0, n_full)
    def _(j):
        step(j, False)

    for d in range(tq // tk):
        step(n_full + d, True)

    lam = lam_ref[0, 0]
    for h in range(n_heads):
        o_maps = acc_sc[h] / l_sc[h]
        o = o_maps[0:tq] - lam * o_maps[tq:2 * tq]
        ms = jnp.mean(o * o, axis=-1, keepdims=True)
        o_ref[0, :, h * LANES:(h + 1) * LANES] = (
            o * lax.rsqrt(ms + NORM_EPS) * g_ref[...] * out_scale).astype(o_ref.dtype)


def _diff_prompt_bounded_kernel(lam_ref, kmax_ref, q_ref, k_ref, v_ref, g_ref, o_ref, q_sc, c_sc, acc_sc,
                                *, tq, tk, out_scale):
    i = pl.program_id(1)
    n_heads = q_sc.shape[0]
    _stack_halves(q_ref, q_sc, tq)
    for h in range(n_heads):
        qf = q_sc[h].astype(F32)
        c_sc[h] = jnp.sqrt(jnp.sum(qf * qf, axis=-1, keepdims=True)) * kmax_ref[0, 0]
    acc_sc[...] = jnp.zeros(acc_sc.shape, F32)
    ones = jnp.ones((tk, LANES), BF16)

    def step(j, masked):
        start = pl.multiple_of(j * tk, tk)
        for h in range(n_heads):
            ls = slice(h * LANES, (h + 1) * LANES)
            kb = k_ref[0, j, ls, :]
            vb = v_ref[0, pl.ds(start, tk), ls]
            s = _dot(q_sc[h], kb)
            if masked:
                row = lax.broadcasted_iota(jnp.int32, s.shape, 0) % tq
                col = lax.broadcasted_iota(jnp.int32, s.shape, 1)
                s = jnp.where(j * tk + col <= i * tq + row, s, NEG_BIG)
            p = jnp.exp(s - c_sc[h]).astype(BF16)
            acc_sc[h] += _dot(p, jnp.concatenate([vb, ones], axis=1))

    n_full = i * (tq // tk)

    @pl.loop(0, n_full)
    def _(j):
        step(j, False)

    for d in range(tq // tk):
        step(n_full + d, True)

    lam = lam_ref[0, 0]
    for h in range(n_heads):
        acc = acc_sc[h]
        o_maps = acc[:, 0:LANES] / acc[:, LANES:2 * LANES]
        o = o_maps[0:tq] - lam * o_maps[tq:2 * tq]
        ms = jnp.mean(o * o, axis=-1, keepdims=True)
        o_ref[0, :, h * LANES:(h + 1) * LANES] = (
            o * lax.rsqrt(ms + NORM_EPS) * g_ref[...] * out_scale).astype(o_ref.dtype)


def _diff_prompt_bounded(lam, kmax, q, k, v, g_subln, out_scale, tq, tk):
    b, s, w = q.shape
    n_heads = w // LANES
    kern = functools.partial(_diff_prompt_bounded_kernel, tq=tq, tk=tk, out_scale=out_scale)
    return pl.pallas_call(
        kern,
        grid=(b, s // tq),
        in_specs=[
            pl.BlockSpec(memory_space=pltpu.SMEM),
            pl.BlockSpec(memory_space=pltpu.SMEM),
            pl.BlockSpec((1, tq, w), lambda bi, i: (bi, i, 0)),
            pl.BlockSpec((1,) + k.shape[1:], lambda bi, i: (bi, 0, 0, 0)),
            pl.BlockSpec((1, s, w), lambda bi, i: (bi, 0, 0)),
            pl.BlockSpec((1, LANES), lambda bi, i: (0, 0)),
        ],
        out_specs=pl.BlockSpec((1, tq, w), lambda bi, i: (bi, i, 0)),
        out_shape=jax.ShapeDtypeStruct((b, s, w), BF16),
        scratch_shapes=[
            pltpu.VMEM((n_heads, 2 * tq, LANES), BF16),
            pltpu.VMEM((n_heads, 2 * tq, 1), F32),
            pltpu.VMEM((n_heads, 2 * tq, 2 * LANES), F32),
        ],
        compiler_params=_cparams(("arbitrary", "arbitrary")),
        name="diff_prompt_bounded",
    )(lam, kmax, q, k, v, g_subln)


def _normed_head_bound(gain):
    return jnp.max(jnp.abs(gain.astype(F32))) * (DH ** 0.5) * BOUND_SLACK


def _diff_prompt(lam, q, k, v, g_subln, out_scale, tq, tk):
    b, s, w = q.shape
    n_heads = w // LANES
    kern = functools.partial(_diff_prompt_kernel, tq=tq, tk=tk, out_scale=out_scale)
    return pl.pallas_call(
        kern,
        grid=(b, s // tq),
        in_specs=[
            pl.BlockSpec(memory_space=pltpu.SMEM),
            pl.BlockSpec((1, tq, w), lambda bi, i: (bi, i, 0)),
            pl.BlockSpec((1,) + k.shape[1:], lambda bi, i: (bi, 0, 0, 0)),
            pl.BlockSpec((1, s, w), lambda bi, i: (bi, 0, 0)),
            pl.BlockSpec((1, LANES), lambda bi, i: (0, 0)),
        ],
        out_specs=pl.BlockSpec((1, tq, w), lambda bi, i: (bi, i, 0)),
        out_shape=jax.ShapeDtypeStruct((b, s, w), BF16),
        scratch_shapes=[
            pltpu.VMEM((n_heads, 2 * tq, LANES), BF16),
            pltpu.VMEM((n_heads, 2 * tq, 1), F32),
            pltpu.VMEM((n_heads, 2 * tq, 1), F32),
            pltpu.VMEM((n_heads, 2 * tq, LANES), F32),
        ],
        compiler_params=_cparams(("arbitrary", "arbitrary")),
        name="diff_prompt",
    )(lam, q, k, v, g_subln)


def _sb_block(z, valid, tri_ref, r_in):
    log_1m = jnp.minimum(-z, 0.0) - jnp.log(1.0 + jnp.exp(-jnp.abs(z)))
    if valid is not None:
        log_1m = jnp.where(valid, log_1m, 0.0)
    later = _dot(jnp.concatenate(_split_bf16(log_1m), axis=1), tri_ref[...])
    a = jnp.exp((z + log_1m) + (later + r_in))
    if valid is not None:
        a = jnp.where(valid, a, 0.0)
    r_out = r_in + later[:, 0:1] + log_1m[:, 0:1]
    return a, r_out


def _head_rms(o, seg_ref, g):
    ms = _dot((o * o).astype(BF16), seg_ref[...])
    return o * lax.rsqrt(ms + NORM_EPS) * g


def _sb_prompt_kernel(q_ref, k_ref, v_ref, g_ref, tri_ref, seg_ref, o_ref, q_sc, r_sc, acc_sc, *, tq, tk):
    i = pl.program_id(1)
    n_blocks = q_sc.shape[0]
    _stack_halves(q_ref, q_sc, tq)
    r_sc[...] = jnp.zeros(r_sc.shape, F32)
    acc_sc[...] = jnp.zeros(acc_sc.shape, F32)

    def step(j, masked):
        start = pl.multiple_of(j * tk, tk)
        r_max = jnp.full((1, 1), NEG_BIG, F32)
        for p in range(n_blocks):
            ls = slice(p * LANES, (p + 1) * LANES)
            kb = k_ref[0, j, ls, :]
            vb = v_ref[0, pl.ds(start, tk), ls]
            z = _dot(q_sc[p], kb)
            valid = None
            if masked:
                row = lax.broadcasted_iota(jnp.int32, z.shape, 0) % tq
                col = lax.broadcasted_iota(jnp.int32, z.shape, 1)
                valid = j * tk + col < i * tq + row
            a, r = _sb_block(z, valid, tri_ref, r_sc[p])
            acc_sc[p] += _dot(a.astype(BF16), vb)
            r_sc[p] = r
            r_max = jnp.maximum(r_max, jnp.max(r, axis=0, keepdims=True))
        return (r_max[0, 0] >= SB_DONE).astype(jnp.int32)

    n_full = i * (tq // tk)
    go = jnp.int32(1)
    for d in reversed(range(tq // tk)):
        go = step(n_full + d, True)

    def not_done(c):
        return jnp.logical_and(c[0] >= 0, c[1] > 0)

    def body(c):
        return c[0] - 1, step(c[0], False)

    lax.while_loop(not_done, body, (n_full - 1, go))
    lo_half = _half_masks((tq, LANES))
    for p in range(n_blocks):
        acc = acc_sc[p]
        o = jnp.where(lo_half, acc[0:tq], acc[tq:2 * tq])
        o_ref[0, :, p * LANES:(p + 1) * LANES] = _head_rms(o, seg_ref, g_ref[...]).astype(o_ref.dtype)


def _sb_prompt(q, k, v, g_sb, tri, seg, tq, tk):
    b, s, w = q.shape
    n_blocks = w // LANES
    kern = functools.partial(_sb_prompt_kernel, tq=tq, tk=tk)
    return pl.pallas_call(
        kern,
        grid=(b, s // tq),
        in_specs=[
            pl.BlockSpec((1, tq, w), lambda bi, i: (bi, i, 0)),
            pl.BlockSpec((1,) + k.shape[1:], lambda bi, i: (bi, 0, 0, 0)),
            pl.BlockSpec((1, s, w), lambda bi, i: (bi, 0, 0)),
            pl.BlockSpec((1, LANES), lambda bi, i: (0, 0)),
            pl.BlockSpec(tri.shape, lambda bi, i: (0, 0)),
            pl.BlockSpec(seg.shape, lambda bi, i: (0, 0)),
        ],
        out_specs=pl.BlockSpec((1, tq, w), lambda bi, i: (bi, i, 0)),
        out_shape=jax.ShapeDtypeStruct((b, s, w), BF16),
        scratch_shapes=[
            pltpu.VMEM((n_blocks, 2 * tq, LANES), BF16),
            pltpu.VMEM((n_blocks, 2 * tq, 1), F32),
            pltpu.VMEM((n_blocks, 2 * tq, LANES), F32),
        ],
        compiler_params=_cparams(("arbitrary", "arbitrary")),
        name="sb_prompt",
    )(q, k, v, g_sb, tri, seg)


Q_ROWS = 16


def _page_spec(n_pages, first_of_chunk, r, page_shape):
    def index_map(b, c, pt_ref):
        return (pt_ref[b * n_pages + first_of_chunk(c) + r], 0, 0)
    return pl.BlockSpec((1,) + page_shape, index_map)


def _page_scores(q, k_refs, rows):
    return _dot(q, jnp.concatenate([r[0, rows, :] for r in k_refs], axis=1).astype(BF16))


def _diff_sample_kernel(pt_ref, lam_ref, q_ref, kn_ref, vn_ref, g_ref, *rest, n_pp, t_new, out_scale):
    k_refs, v_refs = rest[:n_pp], rest[n_pp:2 * n_pp]
    o_ref, m_sc, l_sc, acc_sc = rest[2 * n_pp:]
    c = pl.program_id(1)
    n_heads = q_ref.shape[-1] // LANES

    @pl.when(c == 0)
    def _():
        m_sc[...] = jnp.full(m_sc.shape, NEG_BIG, F32)
        l_sc[...] = jnp.zeros(l_sc.shape, F32)
        acc_sc[...] = jnp.zeros(acc_sc.shape, F32)

    def update(h, s, pv):
        m = m_sc[h]
        m_new = jnp.maximum(m, jnp.max(s, axis=-1, keepdims=True))
        alpha = jnp.exp(m - m_new)
        p = jnp.exp(s - m_new)
        l_sc[h] = alpha * l_sc[h] + jnp.sum(p, axis=-1, keepdims=True)
        acc_sc[h] = alpha * acc_sc[h] + pv(p.astype(BF16))
        m_sc[h] = m_new

    page_size = k_refs[0].shape[-1]
    heads = range(n_heads)
    lanes = [slice(h * LANES, (h + 1) * LANES) for h in heads]
    m_old = [m_sc[h] for h in heads]
    l_old = [l_sc[h] for h in heads]
    acc_old = [acc_sc[h] for h in heads]
    s = [_page_scores(q_ref[0, :, lanes[h]], k_refs, lanes[h]) for h in heads]
    m_new = [jnp.maximum(m_old[h], jnp.max(s[h], axis=-1, keepdims=True)) for h in heads]
    p = [jnp.exp(s[h] - m_new[h]) for h in heads]
    alpha = [jnp.exp(m_old[h] - m_new[h]) for h in heads]
    for h in heads:
        vb = jnp.concatenate([v_r[0, pl.ds(h, page_size, stride=n_heads), :] for v_r in v_refs], axis=0)
        pv = _dot(p[h].astype(BF16), vb.astype(BF16))
        m_sc[h] = m_new[h]
        l_sc[h] = alpha[h] * l_old[h] + jnp.sum(p[h], axis=-1, keepdims=True)
        acc_sc[h] = alpha[h] * acc_old[h] + pv

    @pl.when(c == pl.num_programs(1) - 1)
    def _():
        lam = lam_ref[0, 0]
        row = lax.broadcasted_iota(jnp.int32, (Q_ROWS, Q_ROWS), 0)
        col = lax.broadcasted_iota(jnp.int32, (Q_ROWS, Q_ROWS), 1)
        valid = jnp.logical_and(col < t_new, col <= row % t_new)
        for h in range(n_heads):
            ls = slice(h * LANES, (h + 1) * LANES)
            s = jnp.where(valid, _dot_t(q_ref[0, :, ls], kn_ref[0, :, ls]), NEG_BIG)
            update(h, s, lambda p, ls=ls: _dot(p, vn_ref[0, :, ls]))
            o_maps = acc_sc[h] / l_sc[h]
            o = o_maps[0:8] - lam * o_maps[t_new:t_new + 8]
            ms = jnp.mean(o * o, axis=-1, keepdims=True)
            o_ref[0, :, ls] = o * lax.rsqrt(ms + NORM_EPS) * g_ref[...] * out_scale


def _diff_sample(pt_flat, lam, q16, k_new, v_new, g_subln, cache_k, cache_v, n_pages, n_pp, t_new, out_scale):
    nb, _, w = q16.shape
    n_heads = w // LANES
    per_seq = lambda b, c, pt: (b, 0, 0)
    chunk0 = lambda c: c * n_pp
    pages = lambda cache: [_page_spec(n_pages, chunk0, r, cache.shape[1:]) for r in range(n_pp)]
    kern = functools.partial(_diff_sample_kernel, n_pp=n_pp, t_new=t_new, out_scale=out_scale)
    grid_spec = pltpu.PrefetchScalarGridSpec(
        num_scalar_prefetch=1,
        grid=(nb, n_pages // n_pp),
        in_specs=[
            pl.BlockSpec(memory_space=pltpu.SMEM),
            pl.BlockSpec((1, Q_ROWS, w), per_seq),
            pl.BlockSpec((1, Q_ROWS, w), per_seq),
            pl.BlockSpec((1, Q_ROWS, w), per_seq),
            pl.BlockSpec((1, LANES), lambda b, c, pt: (0, 0)),
        ] + pages(cache_k) + pages(cache_v),
        out_specs=pl.BlockSpec((1, 8, w), per_seq),
        scratch_shapes=[
            pltpu.VMEM((n_heads, Q_ROWS, 1), F32),
            pltpu.VMEM((n_heads, Q_ROWS, 1), F32),
            pltpu.VMEM((n_heads, Q_ROWS, LANES), F32),
        ],
    )
    return pl.pallas_call(
        kern,
        grid_spec=grid_spec,
        out_shape=jax.ShapeDtypeStruct((nb, 8, w), F32),
        compiler_params=_cparams(("arbitrary", "arbitrary")),
        name="diff_sample",
    )(pt_flat, lam, q16, k_new, v_new, g_subln, *([cache_k] * n_pp), *([cache_v] * n_pp))


def _sb_sample_kernel(pt_ref, q_ref, kn_ref, vn_ref, g_ref, tri_ref, trin_ref, seg_ref, acc_in, r_in, *rest,
                      n_pp, t_new, with_new):
    k_refs, v_refs = rest[:n_pp], rest[n_pp:2 * n_pp]
    o_ref, acc_ref, r_ref = rest[2 * n_pp:]
    c = pl.program_id(1)
    n_blocks = q_ref.shape[-1] // LANES

    @pl.when(c == 0)
    def _():
        if with_new:
            row = lax.broadcasted_iota(jnp.int32, (Q_ROWS, Q_ROWS), 0)
            col = lax.broadcasted_iota(jnp.int32, (Q_ROWS, Q_ROWS), 1)
            valid = jnp.logical_and(col < t_new, col < row % t_new)
            for p in range(n_blocks):
                ls = slice(p * LANES, (p + 1) * LANES)
                z = _dot_t(q_ref[0, :, ls], kn_ref[0, :, ls])
                a, r = _sb_block(z, valid, trin_ref, jnp.zeros((Q_ROWS, 1), F32))
                acc_ref[0, :, ls] = _dot(a.astype(BF16), vn_ref[0, :, ls])
                r_ref[0, :, ls] = jnp.broadcast_to(r, (Q_ROWS, LANES))
        else:
            acc_ref[...] = acc_in[...]
            r_ref[...] = r_in[...]

    blocks = range(n_blocks)
    lanes = [slice(p * LANES, (p + 1) * LANES) for p in blocks]
    r_old = [r_ref[0, :, p * LANES:p * LANES + 1] for p in blocks]
    acc_old = [acc_ref[0, :, lanes[p]] for p in blocks]
    z = [_page_scores(q_ref[0, :, lanes[p]], k_refs, lanes[p]) for p in blocks]
    ar = [_sb_block(z[p], None, tri_ref, r_old[p]) for p in blocks]
    for p in blocks:
        vt = jnp.concatenate([v_r[0, lanes[p], :] for v_r in v_refs], axis=1).astype(BF16)
        acc_ref[0, :, lanes[p]] = acc_old[p] + _dot_t(ar[p][0].astype(BF16), vt)
        r_ref[0, :, lanes[p]] = jnp.broadcast_to(ar[p][1], (Q_ROWS, LANES))

    @pl.when(c == pl.num_programs(1) - 1)
    def _():
        lo_half = _half_masks((8, LANES))
        for p in range(n_blocks):
            ls = slice(p * LANES, (p + 1) * LANES)
            acc = acc_ref[0, :, ls]
            o = jnp.where(lo_half, acc[0:8], acc[t_new:t_new + 8])
            o_ref[0, :, ls] = _head_rms(o, seg_ref, g_ref[...])


def _sb_sample(pt_flat, q16, k_new, v_new, g_sb, tri, tri_new, seg, acc_in, r_in, cache_k, cache_v,
               n_pages, page_hi, n_chunks, n_pp, t_new, with_new):
    nb, _, w = q16.shape
    per_seq = lambda b, c, pt: (b, 0, 0)
    fixed = lambda b, c, pt: (0, 0)
    first = lambda c: page_hi - (c + 1) * n_pp
    pages = lambda cache: [_page_spec(n_pages, first, r, cache.shape[1:]) for r in range(n_pp)]
    kern = functools.partial(_sb_sample_kernel, n_pp=n_pp, t_new=t_new, with_new=with_new)
    state_spec = pl.BlockSpec((1, Q_ROWS, w), per_seq)
    grid_spec = pltpu.PrefetchScalarGridSpec(
        num_scalar_prefetch=1,
        grid=(nb, n_chunks),
        in_specs=[
            state_spec, state_spec, state_spec,
            pl.BlockSpec((1, LANES), fixed),
            pl.BlockSpec(tri.shape, fixed),
            pl.BlockSpec(tri_new.shape, fixed),
            pl.BlockSpec(seg.shape, fixed),
            state_spec, state_spec,
        ] + pages(cache_k) + pages(cache_v),
        out_specs=[pl.BlockSpec((1, 8, w), per_seq), state_spec, state_spec],
    )
    state = jax.ShapeDtypeStruct((nb, Q_ROWS, w), F32)
    return pl.pallas_call(
        kern,
        grid_spec=grid_spec,
        out_shape=[jax.ShapeDtypeStruct((nb, 8, w), F32), state, state],
        compiler_params=_cparams(("arbitrary", "arbitrary")),
        name="sb_sample_new" if with_new else "sb_sample_rest",
    )(pt_flat, q16, k_new, v_new, g_sb, tri, tri_new, seg, acc_in, r_in,
      *([cache_k] * n_pp), *([cache_v] * n_pp))


def _pack_bf16_pairs(x):
    c = x.shape[-1] // 2
    bits = lax.bitcast_convert_type(x.astype(F32), jnp.uint32)
    return lax.bitcast_convert_type((bits[:, :c] >> 16) | (bits[:, c:] & jnp.uint32(0xFFFF0000)), F32)


def _unpack_bf16_pairs(words):
    w = lax.bitcast_convert_type(words, jnp.uint32)
    lo = lax.bitcast_convert_type(w << 16, F32)
    hi = lax.bitcast_convert_type(w & jnp.uint32(0xFFFF0000), F32)
    return jnp.concatenate([lo, hi], axis=1).astype(BF16)


def _finish_kernel(x_ref, od_ref, os_ref, wo_ref, g_ref, wr_hi_ref, wr_lo_ref, br_ref,
                   h_ref, hn_ref, route_ref):
    half = od_ref.shape[-1]
    h = x_ref[...] + _dot(od_ref[...], wo_ref[0:half, :]) + _dot(os_ref[...], wo_ref[half:, :])
    h_ref[...] = h
    ms = jnp.mean(h * h, axis=-1, keepdims=True)
    hn = h * lax.rsqrt(ms + NORM_EPS) * g_ref[...]
    hn_hi, hn_lo = _split_bf16(hn)
    hn_ref[...] = _pack_bf16_pairs(hn_hi)

    logits = (_dot(hn_hi, wr_hi_ref[...]) + _dot(hn_lo, wr_hi_ref[...]) + _dot(hn_hi, wr_lo_ref[...])
              + br_ref[...])
    n_exp = N_GROUPS * EXPERTS_PER_GROUP
    lane = lax.broadcasted_iota(jnp.int32, logits.shape, 1)

    def first_argmax(v, vmax):
        return jnp.min(jnp.where(v == vmax, lane, LANES), axis=-1, keepdims=True)

    lg = jnp.where(jnp.logical_and(lane >= n_exp, lane < n_exp + N_GROUPS), logits, NEG_BIG)
    mg = jnp.max(lg, axis=-1, keepdims=True)
    grp = first_argmax(lg, mg) - n_exp
    p_top = 1.0 / jnp.sum(jnp.exp(lg - mg), axis=-1, keepdims=True)

    le = jnp.where(jnp.logical_and(lane < n_exp, lane // EXPERTS_PER_GROUP == grp), logits, NEG_BIG)
    m1 = jnp.max(le, axis=-1, keepdims=True)
    i1 = first_argmax(le, m1)
    le2 = jnp.where(lane == i1, NEG_BIG, le)
    m2 = jnp.max(le2, axis=-1, keepdims=True)
    i2 = first_argmax(le2, m2)
    e2 = jnp.exp(m2 - m1)
    gate1 = p_top / (1.0 + e2)
    gate2 = p_top * e2 / (1.0 + e2)
    route = jnp.where(lane == 0, i1.astype(F32),
                      jnp.where(lane == 1, i2.astype(F32),
                                jnp.where(lane == 2, gate1, jnp.where(lane == 3, gate2, 0.0))))
    route_ref[...] = route


def _finish(x2d, od, osb, w_o_bf, g_ffn, wr_hi, wr_lo, b_r, tm):
    n, d = x2d.shape
    half = od.shape[-1]
    row = lambda i: (i, 0)
    fixed = lambda i: (0, 0)
    return pl.pallas_call(
        _finish_kernel,
        grid=(n // tm,),
        in_specs=[
            pl.BlockSpec((tm, d), row),
            pl.BlockSpec((tm, half), row),
            pl.BlockSpec((tm, half), row),
            pl.BlockSpec(w_o_bf.shape, fixed),
            pl.BlockSpec((1, d), fixed),
            pl.BlockSpec(wr_hi.shape, fixed),
            pl.BlockSpec(wr_lo.shape, fixed),
            pl.BlockSpec((1, LANES), fixed),
        ],
        out_specs=[pl.BlockSpec((tm, d), row), pl.BlockSpec((tm, d // 2), row), pl.BlockSpec((tm, LANES), row)],
        out_shape=[jax.ShapeDtypeStruct((n, d), F32), jax.ShapeDtypeStruct((n, d // 2), F32),
                   jax.ShapeDtypeStruct((n, LANES), F32)],
        compiler_params=_cparams(("arbitrary",)),
        name="out_proj_router",
    )(x2d, od, osb, w_o_bf, g_ffn, wr_hi, wr_lo, b_r)


def _expert_kernel(be_ref, nact_ref, x_ref, wg_ref, wu_ref, wd_ref, y_ref, wg_sc, wu_sc, wd_sc):
    blk = pl.program_id(0)
    active = blk < nact_ref[0]

    @pl.when(jnp.logical_and(active, jnp.logical_or(blk == 0, be_ref[blk] != be_ref[jnp.maximum(blk - 1, 0)])))
    def _():
        wg_sc[...] = wg_ref[0].astype(BF16)
        wu_sc[...] = wu_ref[0].astype(BF16)
        wd_sc[...] = wd_ref[0].astype(BF16)

    @pl.when(active)
    def _():
        x = _unpack_bf16_pairs(x_ref[...])
        gate = _dot(x, wg_sc[...])
        up = _dot(x, wu_sc[...])
        hidden = (gate * jax.nn.sigmoid(gate) * up).astype(BF16)
        y_ref[...] = _dot(hidden, wd_sc[...])

    @pl.when(blk >= nact_ref[0])
    def _():
        y_ref[...] = jnp.zeros(y_ref.shape, y_ref.dtype)


def _expert_ffn(block_exp, n_active, xs, w_gate, w_up, w_down, bm):
    n_slots = xs.shape[0]
    d, ff = w_gate.shape[-2:]
    grid_spec = pltpu.PrefetchScalarGridSpec(
        num_scalar_prefetch=2,
        grid=(n_slots // bm,),
        in_specs=[
            pl.BlockSpec((bm, xs.shape[1]), lambda i, be, na: (i, 0)),
            pl.BlockSpec((1, d, ff), lambda i, be, na: (be[i], 0, 0)),
            pl.BlockSpec((1, d, ff), lambda i, be, na: (be[i], 0, 0)),
            pl.BlockSpec((1, ff, d), lambda i, be, na: (be[i], 0, 0)),
        ],
        out_specs=pl.BlockSpec((bm, d), lambda i, be, na: (i, 0)),
        scratch_shapes=[pltpu.VMEM((d, ff), BF16), pltpu.VMEM((d, ff), BF16), pltpu.VMEM((ff, d), BF16)],
    )
    return pl.pallas_call(
        _expert_kernel,
        grid_spec=grid_spec,
        out_shape=jax.ShapeDtypeStruct((n_slots, d), F32),
        compiler_params=_cparams(("arbitrary",)),
        name="expert_ffn",
    )(block_exp, n_active, xs, w_gate, w_up, w_down)


def _moe(h_parts, hn_packed, route, w_gate, w_up, w_down, bm):
    n_tok = hn_packed.shape[0]
    n_exp = w_gate.shape[0]
    ids = route[:, 0:2].astype(jnp.int32)
    gates = route[:, 2:4]
    n_assign = 2 * n_tok
    flat_e = ids.reshape(-1)
    experts = jnp.arange(n_exp, dtype=jnp.int32)
    order = jnp.argsort(flat_e, stable=True).astype(jnp.int32)
    rank = jnp.argsort(order).astype(jnp.int32)
    counts = jnp.sum((flat_e[:, None] == experts[None, :]).astype(jnp.int32), axis=0)
    padded = (counts + bm - 1) // bm * bm
    start = jnp.cumsum(counts) - counts
    pend = jnp.cumsum(padded)
    pstart = pend - padded
    dest = (pstart[flat_e] + rank - start[flat_e]).reshape(n_tok, 2)
    n_blocks = -(-n_assign // bm) + n_exp
    blk_start = jnp.arange(n_blocks, dtype=jnp.int32) * bm
    block_exp = jnp.minimum(jnp.sum((blk_start[:, None] >= pend[None, :]).astype(jnp.int32), axis=1), n_exp - 1)
    n_active = (pend[-1:] // bm).astype(jnp.int32)
    slot_exp = jnp.repeat(block_exp, bm)
    offset = jnp.arange(n_blocks * bm, dtype=jnp.int32) - pstart[slot_exp]
    src = jnp.minimum(start[slot_exp] + offset, n_assign - 1)
    slot_tok = jnp.where(offset < counts[slot_exp], order[src] // 2, 0)
    xs = hn_packed[slot_tok]
    yb = _expert_ffn(block_exp, n_active, xs, w_gate, w_up, w_down, bm)
    outs, lo = [], 0
    for h in h_parts:
        hi = lo + h.shape[0]
        outs.append(h + (gates[lo:hi, 0:1] * yb[dest[lo:hi, 0]] + gates[lo:hi, 1:2] * yb[dest[lo:hi, 1]]))
        lo = hi
    return outs


def _block_diag_mean(width, group):
    idx = jnp.arange(width) // group
    return jnp.where(idx[:, None] == idx[None, :], 1.0 / group, 0.0).astype(BF16)


def _tri(n):
    idx = jnp.arange(n)
    tri = (idx[:, None] > idx[None, :]).astype(BF16)
    return jnp.concatenate([tri, tri], axis=0)


def _sample_rows(a, nb, t):
    w = a.shape[-1]
    a = a.reshape(nb, t, w)
    lo = (jnp.arange(w) % LANES) < DH
    zeros = jnp.zeros((nb, Q_ROWS - 2 * t, w), a.dtype)
    return jnp.concatenate([jnp.where(lo, a, 0), jnp.where(lo, 0, a), zeros], axis=1)


def _pad_rows(a, nb, t):
    w = a.shape[-1]
    a = a.reshape(nb, t, w)
    return jnp.concatenate([a, jnp.zeros((nb, Q_ROWS - t, w), a.dtype)], axis=1)


def kernel(x_prompt, x_sample, cache_k_diff, cache_v_diff, cache_k_sb, cache_v_sb, page_table, g_attn_norm, w_in, g_q_norm, g_k_norm, lambda_q1, lambda_k1, lambda_q2, lambda_k2, g_subln, g_sb_out, w_o, g_ffn_norm, w_router_group, b_router_group, w_router_expert, b_router_expert, w_exp_gate, w_exp_up, w_exp_down):
    depth = w_in.shape[0]
    b, s, d = x_prompt.shape
    nb, t_new, _ = x_sample.shape
    n_phys, page_size = cache_k_diff.shape[1], cache_k_diff.shape[2]
    n_pages = page_table.shape[1]
    past_len = n_pages * page_size
    width = w_in.shape[-1] // 6
    n_exp = w_exp_gate.shape[1]

    tm = min(256, nb * t_new)
    tq, tk = min(PROMPT_TQ, s), min(PROMPT_TK, s)
    n_pp_diff = min(16, n_pages)
    n_pp_sb = min(2, n_pages)
    bm = 256

    seg64 = _block_diag_mean(width, DH)
    seg64_lane = _block_diag_mean(LANES, DH)
    tri_p = _tri(tk)
    tri_s = _tri(n_pp_sb * page_size)
    tri_new = _tri(Q_ROWS)
    pos_p = _rope_tables(jnp.arange(s), 1)
    pos_s = _rope_tables(past_len + jnp.arange(t_new), tm // t_new)

    hp = x_prompt.reshape(b * s, d)
    hs = x_sample.reshape(nb * t_new, d)
    outs = [[] for _ in range(8)]
    for layer in range(depth):
        lam_init = 0.8 - 0.6 * math.exp(-0.3 * layer)
        lam = (jnp.exp(jnp.sum(lambda_q1[layer].astype(F32) * lambda_k1[layer].astype(F32)))
               - jnp.exp(jnp.sum(lambda_q2[layer].astype(F32) * lambda_k2[layer].astype(F32))) + lam_init)
        lam = lam.reshape(1, 1).astype(F32)
        out_scale = 1.0 - lam_init
        g_attn = g_attn_norm[layer].reshape(1, d)
        w_in_bf = w_in[layer].astype(BF16)
        gq = jnp.tile(g_q_norm[layer], width // DH).reshape(1, width)
        gk = jnp.tile(g_k_norm[layer], width // DH).reshape(1, width)
        g_sub = g_subln[layer].reshape(1, LANES)
        g_sb = jnp.tile(g_sb_out[layer], LANES // DH).reshape(1, LANES)

        (kd_p, vd_p, ks_p, vs_p, qd_pb, kd_pb, vd_pb, qs_pb, ks_pb, vs_pb) = _project(
            hp, pos_p, g_attn, w_in_bf, gq, gk, seg64, tk, seq_len=s)
        (kd_s, vd_s, ks_s, vs_s, qd_sb, kd_sb, vd_sb, qs_sb, ks_sb, vs_sb) = _project(
            hs, pos_s, g_attn, w_in_bf, gq, gk, seg64, tm)

        r3 = lambda a: a.reshape(b, s, width)
        qd3, kd3, vd3 = r3(qd_pb), kd_pb, r3(vd_pb)
        qmax = _normed_head_bound(g_q_norm[layer]) * (DH ** -0.5)
        kmax = _normed_head_bound(g_k_norm[layer])
        od_p = lax.cond(
            qmax * kmax <= BOUND_MAX,
            lambda: _diff_prompt_bounded(lam, kmax.reshape(1, 1), qd3, kd3, vd3, g_sub, out_scale, tq, tk),
            lambda: _diff_prompt(lam, qd3, kd3, vd3, g_sub, out_scale, tq, tk))
        osb_p = _sb_prompt(r3(qs_pb), ks_pb, r3(vs_pb), g_sb, tri_p, seg64_lane, tq, tk)

        pt_flat = (page_table.astype(jnp.int32) + layer * n_phys).reshape(-1)
        transposed_pages = lambda c: jnp.transpose(c, (0, 1, 3, 4, 2)).reshape(depth * n_phys, width, page_size)
        v_diff_pages = cache_v_diff.reshape(depth * n_phys, page_size * (width // DV_DIFF_W), DV_DIFF_W)
        od_s = _diff_sample(pt_flat, lam, _sample_rows(qd_sb, nb, t_new), _pad_rows(kd_sb, nb, t_new),
                            _pad_rows(vd_sb, nb, t_new), g_sub, transposed_pages(cache_k_diff),
                            v_diff_pages, n_pages, n_pp_diff, t_new, out_scale)
        q16s = _sample_rows(qs_sb, nb, t_new)
        kn16, vn16 = _pad_rows(ks_sb, nb, t_new), _pad_rows(vs_sb, nb, t_new)
        cks, cvs = transposed_pages(cache_k_sb), transposed_pages(cache_v_sb)
        zeros_state = jnp.zeros((nb, Q_ROWS, width), F32)
        sb_args = (pt_flat, q16s, kn16, vn16, g_sb, tri_s, tri_new, seg64_lane)
        osb_s, acc_s, r_s = _sb_sample(*sb_args, zeros_state, zeros_state, cks, cvs,
                                       n_pages, n_pages, 1, n_pp_sb, t_new, True)
        n_rest = (n_pages - n_pp_sb) // n_pp_sb
        if n_rest > 0:
            unfinished = jnp.max(r_s[:, :2 * t_new, :]) >= SB_DONE
            osb_s = lax.cond(
                unfinished,
                lambda: _sb_sample(*sb_args, acc_s, r_s, cks, cvs, n_pages, n_pages - n_pp_sb,
                                   n_rest, n_pp_sb, t_new, False)[0],
                lambda: osb_s)

        w_o_bf = w_o[layer].astype(BF16)
        g_ffn = g_ffn_norm[layer].reshape(1, d)
        w_r = jnp.zeros((d, LANES), F32)
        w_r = w_r.at[:, :n_exp].set(w_router_expert[layer]).at[:, n_exp:n_exp + N_GROUPS].set(w_router_group[layer])
        wr_hi, wr_lo = _split_bf16(w_r)
        b_r = jnp.zeros((1, LANES), F32)
        b_r = b_r.at[0, :n_exp].set(b_router_expert[layer]).at[0, n_exp:n_exp + N_GROUPS].set(b_router_group[layer])

        od_s2 = od_s[:, :t_new].reshape(nb * t_new, width).astype(BF16)
        osb_s2 = osb_s[:, :t_new].reshape(nb * t_new, width).astype(BF16)
        fin = lambda xx, od, osb, tile: _finish(xx, od, osb, w_o_bf, g_ffn, wr_hi, wr_lo, b_r, tile)
        h_p, hn_p, route_p = fin(hp, od_p.reshape(b * s, width), osb_p.reshape(b * s, width), min(256, s))
        h_s, hn_s, route_s = fin(hs, od_s2, osb_s2, tm)

        hp, hs = _moe((h_p, h_s), jnp.concatenate([hn_p, hn_s]), jnp.concatenate([route_p, route_s]),
                      w_exp_gate[layer], w_exp_up[layer], w_exp_down[layer], bm)

        heads_last = lambda a: jnp.transpose(a.reshape(b, width // DH, DH, s), (0, 3, 1, 2))
        for lst, val in zip(outs, (heads_last(kd_p), vd_p, heads_last(ks_p), heads_last(vs_p),
                                   kd_s, vd_s, ks_s, vs_s)):
            lst.append(val)

    h_diff, h_sb = width // DV_DIFF_W, width // DH
    shapes_p = [(b, s, 2 * h_diff, DH), (b, s, h_diff, DV_DIFF_W), (b, s, h_sb, DH), (b, s, h_sb, DH)]
    shapes_s = [(nb, t_new) + sh[2:] for sh in shapes_p]
    stacked = [jnp.stack([v.reshape(sh) for v in lst]) for lst, sh in zip(outs, shapes_p + shapes_s)]
    return (hp.reshape(b, s, d), hs.reshape(nb, t_new, d), *stacked)
```

```python
import functools
import math

import jax
import jax.numpy as jnp
from jax import lax
from jax.experimental import pallas as pl
from jax.experimental.pallas import tpu as pltpu

F32 = jnp.float32
BF16 = jnp.bfloat16

NORM_EPS = 1e-6
ROPE_THETA = 10000.0
DH = 64
DV_DIFF_W = 2 * DH
LANES = 128
N_GROUPS = 4
EXPERTS_PER_GROUP = 8
NEG_BIG = -1e30
SB_DONE = -110.0
VMEM_LIMIT = 56 * 1024 * 1024
BOUND_MAX = 40.0
BOUND_SLACK = 1.01
PROMPT_TQ = 256
PROMPT_TK = 256


def _cparams(sem):
    return pltpu.CompilerParams(dimension_semantics=sem, vmem_limit_bytes=VMEM_LIMIT)


def _dot_t(a, b):
    return lax.dot_general(a, b, (((1,), (1,)), ((), ())), preferred_element_type=F32)


def _dot(a, b):
    return jnp.dot(a, b, preferred_element_type=F32)


def _split_bf16(x):
    hi = x.astype(BF16)
    lo = (x - hi.astype(F32)).astype(BF16)
    return hi, lo


def _proj_kernel(x_ref, g_ref, w_ref, gq_ref, gk_ref, cos_ref, sin_ref, seg_ref,
                 kd_ref, vd_ref, ks_ref, vs_ref,
                 qd_bf, kd_bf, vd_bf, qs_bf, ks_bf, vs_bf, *, transposed):
    x = x_ref[...]
    ms = jnp.mean(x * x, axis=-1, keepdims=True)
    xn = (x * lax.rsqrt(ms + NORM_EPS) * g_ref[...]).astype(BF16)
    width = vd_bf.shape[-1]

    def put_key_like(f32_ref, bf_ref, val):
        if transposed:
            val_t = val.T
            f32_ref[0] = val_t
            if bf_ref is not None:
                bf_ref[0, 0] = val_t.astype(BF16)
        else:
            f32_ref[...] = val
            if bf_ref is not None:
                bf_ref[...] = val.astype(BF16)

    def seg(c):
        return _dot(xn, w_ref[:, c * width:(c + 1) * width])

    reps = width // LANES
    cos = jnp.concatenate([cos_ref[...]] * reps, axis=1)
    sin = jnp.concatenate([sin_ref[...]] * reps, axis=1)
    lane = lax.broadcasted_iota(jnp.int32, (x.shape[0], width), 1)
    first_half = (lane % DH) < (DH // 2)

    def norm_rope(p, g):
        ms_h = _dot((p * p).astype(BF16), seg_ref[...])
        y = p * lax.rsqrt(ms_h + NORM_EPS) * g
        partner = jnp.where(first_half, pltpu.roll(y, width - DH // 2, 1), pltpu.roll(y, DH // 2, 1))
        return y * cos + partner * sin

    scale = DH ** -0.5
    qd = norm_rope(seg(0), gq_ref[...])
    qd_bf[...] = (qd * scale).astype(BF16)
    put_key_like(kd_ref, kd_bf, norm_rope(seg(1), gk_ref[...]))
    vd = seg(2)
    if transposed:
        n_vh = width // DV_DIFF_W
        for h in range(n_vh):
            vd_ref[pl.ds(h, x.shape[0], stride=n_vh), :] = vd[:, h * DV_DIFF_W:(h + 1) * DV_DIFF_W]
    else:
        vd_ref[...] = vd
    vd_bf[...] = vd.astype(BF16)
    qs_bf[...] = (seg(3) * scale).astype(BF16)
    put_key_like(ks_ref, ks_bf, seg(4))
    vs = seg(5)
    put_key_like(vs_ref, None, vs)
    vs_bf[...] = vs.astype(BF16)


def _rope_tables(pos, reps):
    half = DH // 2
    inv = jnp.power(jnp.float32(ROPE_THETA), -(jnp.arange(half, dtype=F32) * 2.0 / DH))
    ang = pos.astype(F32)[:, None] * inv[None, :]
    cos, sin = jnp.cos(ang), jnp.sin(ang)
    cos_t = jnp.concatenate([cos, cos] * (LANES // DH), axis=1)
    sin_t = jnp.concatenate([-sin, sin] * (LANES // DH), axis=1)
    return jnp.tile(cos_t, (reps, 1)), jnp.tile(sin_t, (reps, 1))


def _project(x2d, pos_tables, g_attn, w_in_bf, gq, gk, seg_mat, tm, seq_len=None):
    n, d = x2d.shape
    width = w_in_bf.shape[1] // 6
    cos_t, sin_t = pos_tables
    n_pos_blocks = cos_t.shape[0] // tm
    row = lambda i: (i, 0)
    fixed = lambda i: (0, 0)
    out_f32 = jax.ShapeDtypeStruct((n, width), F32)
    out_bf = jax.ShapeDtypeStruct((n, width), BF16)
    natural = pl.BlockSpec((tm, width), row)
    out_specs = [natural] * 10
    out_shape = [out_f32] * 4 + [out_bf] * 6
    if seq_len is not None:
        n_sb = seq_len // tm
        key_f32 = jax.ShapeDtypeStruct((n // seq_len, width, seq_len), F32)
        key_bf = jax.ShapeDtypeStruct((n // seq_len, n_sb, width, tm), BF16)
        spec_f32 = pl.BlockSpec((1, width, tm), lambda i: (i // n_sb, 0, i % n_sb))
        spec_bf = pl.BlockSpec((1, 1, width, tm), lambda i: (i // n_sb, i % n_sb, 0, 0))
        n_vh = width // DV_DIFF_W
        val_f32 = jax.ShapeDtypeStruct((n * n_vh, DV_DIFF_W), F32)
        spec_val = pl.BlockSpec((tm * n_vh, DV_DIFF_W), row)
        out_specs = [spec_f32, spec_val, spec_f32, spec_f32, natural, spec_bf, natural, natural, spec_bf, natural]
        out_shape = [key_f32, val_f32, key_f32, key_f32, out_bf, key_bf, out_bf, out_bf, key_bf, out_bf]
    return pl.pallas_call(
        functools.partial(_proj_kernel, transposed=seq_len is not None),
        grid=(n // tm,),
        in_specs=[
            pl.BlockSpec((tm, d), row),
            pl.BlockSpec((1, d), fixed),
            pl.BlockSpec(w_in_bf.shape, fixed),
            pl.BlockSpec((1, width), fixed),
            pl.BlockSpec((1, width), fixed),
            pl.BlockSpec((tm, LANES), lambda i: (i % n_pos_blocks, 0)),
            pl.BlockSpec((tm, LANES), lambda i: (i % n_pos_blocks, 0)),
            pl.BlockSpec(seg_mat.shape, fixed),
        ],
        out_specs=out_specs,
        out_shape=out_shape,
        compiler_params=_cparams(("arbitrary",)),
        name="in_proj",
    )(x2d, g_attn, w_in_bf, gq, gk, cos_t, sin_t, seg_mat)


def _half_masks(shape):
    lane = lax.broadcasted_iota(jnp.int32, shape, 1)
    return lane < DH


def _stack_halves(q_ref, q_sc, tq):
    lo_half = _half_masks((tq, LANES))
    zero = jnp.zeros((), BF16)
    for h in range(q_sc.shape[0]):
        qh = q_ref[0, :, h * LANES:(h + 1) * LANES]
        q_sc[h, 0:tq, :] = jnp.where(lo_half, qh, zero)
        q_sc[h, tq:2 * tq, :] = jnp.where(lo_half, zero, qh)


def _diff_prompt_kernel(lam_ref, q_ref, k_ref, v_ref, g_ref, o_ref, q_sc, m_sc, l_sc, acc_sc,
                        *, tq, tk, out_scale):
    i = pl.program_id(1)
    n_heads = q_sc.shape[0]
    _stack_halves(q_ref, q_sc, tq)
    m_sc[...] = jnp.full(m_sc.shape, NEG_BIG, F32)
    l_sc[...] = jnp.zeros(l_sc.shape, F32)
    acc_sc[...] = jnp.zeros(acc_sc.shape, F32)

    def step(j, masked):
        start = pl.multiple_of(j * tk, tk)
        for h in range(n_heads):
            ls = slice(h * LANES, (h + 1) * LANES)
            kb = k_ref[0, j, ls, :]
            vb = v_ref[0, pl.ds(start, tk), ls]
            s = _dot(q_sc[h], kb)
            if masked:
                row = lax.broadcasted_iota(jnp.int32, s.shape, 0) % tq
                col = lax.broadcasted_iota(jnp.int32, s.shape, 1)
                s = jnp.where(j * tk + col <= i * tq + row, s, NEG_BIG)
            m = m_sc[h]
            m_new = jnp.maximum(m, jnp.max(s, axis=-1, keepdims=True))
            alpha = jnp.exp(m - m_new)
            p = jnp.exp(s - m_new)
            l_sc[h] = alpha * l_sc[h] + jnp.sum(p, axis=-1, keepdims=True)
            acc_sc[h] = alpha * acc_sc[h] + _dot(p.astype(BF16), vb)
            m_sc[h] = m_new

    n_full = i * (tq // tk)

    @pl.loop(0, n_full)
    def _(j):
        step(j, False)

    for d in range(tq // tk):
        step(n_full + d, True)

    lam = lam_ref[0, 0]
    for h in range(n_heads):
        o_maps = acc_sc[h] / l_sc[h]
        o = o_maps[0:tq] - lam * o_maps[tq:2 * tq]
        ms = jnp.mean(o * o, axis=-1, keepdims=True)
        o_ref[0, :, h * LANES:(h + 1) * LANES] = (
            o * lax.rsqrt(ms + NORM_EPS) * g_ref[...] * out_scale).astype(o_ref.dtype)


def _diff_prompt_bounded_kernel(lam_ref, kmax_ref, q_ref, k_ref, v_ref, g_ref, o_ref, q_sc, c_sc, acc_sc,
                                *, tq, tk, out_scale):
    i = pl.program_id(1)
    n_heads = q_sc.shape[0]
    _stack_halves(q_ref, q_sc, tq)
    for h in range(n_heads):
        qf = q_sc[h].astype(F32)
        c_sc[h] = jnp.sqrt(jnp.sum(qf * qf, axis=-1, keepdims=True)) * kmax_ref[0, 0]
    acc_sc[...] = jnp.zeros(acc_sc.shape, F32)
    ones = jnp.ones((tk, LANES), BF16)

    def step(j, masked):
        start = pl.multiple_of(j * tk, tk)
        heads = range(n_heads)
        lanes = [slice(h * LANES, (h + 1) * LANES) for h in heads]
        acc_old = [acc_sc[h] for h in heads]
        s = [_dot(q_sc[h], k_ref[0, j, lanes[h], :]) for h in heads]
        if masked:
            row = lax.broadcasted_iota(jnp.int32, s[0].shape, 0) % tq
            col = lax.broadcasted_iota(jnp.int32, s[0].shape, 1)
            s = [jnp.where(j * tk + col <= i * tq + row, s[h], NEG_BIG) for h in heads]
        p = [jnp.exp(s[h] - c_sc[h]).astype(BF16) for h in heads]
        for h in heads:
            vb = v_ref[0, pl.ds(start, tk), lanes[h]]
            acc_sc[h] = acc_old[h] + _dot(p[h], jnp.concatenate([vb, ones], axis=1))

    n_full = i * (tq // tk)

    @pl.loop(0, n_full)
    def _(j):
        step(j, False)

    for d in range(tq // tk):
        step(n_full + d, True)

    lam = lam_ref[0, 0]
    for h in range(n_heads):
        acc = acc_sc[h]
        o_maps = acc[:, 0:LANES] / acc[:, LANES:2 * LANES]
        o = o_maps[0:tq] - lam * o_maps[tq:2 * tq]
        ms = jnp.mean(o * o, axis=-1, keepdims=True)
        o_ref[0, :, h * LANES:(h + 1) * LANES] = (
            o * lax.rsqrt(ms + NORM_EPS) * g_ref[...] * out_scale).astype(o_ref.dtype)


def _diff_prompt_bounded(lam, kmax, q, k, v, g_subln, out_scale, tq, tk):
    b, s, w = q.shape
    n_heads = w // LANES
    kern = functools.partial(_diff_prompt_bounded_kernel, tq=tq, tk=tk, out_scale=out_scale)
    return pl.pallas_call(
        kern,
        grid=(b, s // tq),
        in_specs=[
            pl.BlockSpec(memory_space=pltpu.SMEM),
            pl.BlockSpec(memory_space=pltpu.SMEM),
            pl.BlockSpec((1, tq, w), lambda bi, i: (bi, i, 0)),
            pl.BlockSpec((1,) + k.shape[1:], lambda bi, i: (bi, 0, 0, 0)),
            pl.BlockSpec((1, s, w), lambda bi, i: (bi, 0, 0)),
            pl.BlockSpec((1, LANES), lambda bi, i: (0, 0)),
        ],
        out_specs=pl.BlockSpec((1, tq, w), lambda bi, i: (bi, i, 0)),
        out_shape=jax.ShapeDtypeStruct((b, s, w), BF16),
        scratch_shapes=[
            pltpu.VMEM((n_heads, 2 * tq, LANES), BF16),
            pltpu.VMEM((n_heads, 2 * tq, 1), F32),
            pltpu.VMEM((n_heads, 2 * tq, 2 * LANES), F32),
        ],
        compiler_params=_cparams(("arbitrary", "arbitrary")),
        name="diff_prompt_bounded",
    )(lam, kmax, q, k, v, g_subln)


def _normed_head_bound(gain):
    return jnp.max(jnp.abs(gain.astype(F32))) * (DH ** 0.5) * BOUND_SLACK


def _diff_prompt(lam, q, k, v, g_subln, out_scale, tq, tk):
    b, s, w = q.shape
    n_heads = w // LANES
    kern = functools.partial(_diff_prompt_kernel, tq=tq, tk=tk, out_scale=out_scale)
    return pl.pallas_call(
        kern,
        grid=(b, s // tq),
        in_specs=[
            pl.BlockSpec(memory_space=pltpu.SMEM),
            pl.BlockSpec((1, tq, w), lambda bi, i: (bi, i, 0)),
            pl.BlockSpec((1,) + k.shape[1:], lambda bi, i: (bi, 0, 0, 0)),
            pl.BlockSpec((1, s, w), lambda bi, i: (bi, 0, 0)),
            pl.BlockSpec((1, LANES), lambda bi, i: (0, 0)),
        ],
        out_specs=pl.BlockSpec((1, tq, w), lambda bi, i: (bi, i, 0)),
        out_shape=jax.ShapeDtypeStruct((b, s, w), BF16),
        scratch_shapes=[
            pltpu.VMEM((n_heads, 2 * tq, LANES), BF16),
            pltpu.VMEM((n_heads, 2 * tq, 1), F32),
            pltpu.VMEM((n_heads, 2 * tq, 1), F32),
            pltpu.VMEM((n_heads, 2 * tq, LANES), F32),
        ],
        compiler_params=_cparams(("arbitrary", "arbitrary")),
        name="diff_prompt",
    )(lam, q, k, v, g_subln)


def _sb_block(z, valid, tri_ref, r_in):
    log_1m = jnp.minimum(-z, 0.0) - jnp.log(1.0 + jnp.exp(-jnp.abs(z)))
    if valid is not None:
        log_1m = jnp.where(valid, log_1m, 0.0)
    later = _dot(jnp.concatenate(_split_bf16(log_1m), axis=1), tri_ref[...])
    a = jnp.exp((z + log_1m) + (later + r_in))
    if valid is not None:
        a = jnp.where(valid, a, 0.0)
    r_out = r_in + later[:, 0:1] + log_1m[:, 0:1]
    return a, r_out


def _head_rms(o, seg_ref, g):
    ms = _dot((o * o).astype(BF16), seg_ref[...])
    return o * lax.rsqrt(ms + NORM_EPS) * g


def _sb_prompt_kernel(q_ref, k_ref, v_ref, g_ref, tri_ref, seg_ref, o_ref, q_sc, r_sc, acc_sc, *, tq, tk):
    i = pl.program_id(1)
    n_blocks = q_sc.shape[0]
    _stack_halves(q_ref, q_sc, tq)
    r_sc[...] = jnp.zeros(r_sc.shape, F32)
    acc_sc[...] = jnp.zeros(acc_sc.shape, F32)

    def step(j, masked):
        start = pl.multiple_of(j * tk, tk)
        blocks = range(n_blocks)
        lanes = [slice(p * LANES, (p + 1) * LANES) for p in blocks]
        r_old = [r_sc[p] for p in blocks]
        acc_old = [acc_sc[p] for p in blocks]
        z = [_dot(q_sc[p], k_ref[0, j, lanes[p], :]) for p in blocks]
        valid = None
        if masked:
            row = lax.broadcasted_iota(jnp.int32, z[0].shape, 0) % tq
            col = lax.broadcasted_iota(jnp.int32, z[0].shape, 1)
            valid = j * tk + col < i * tq + row
        ar = [_sb_block(z[p], valid, tri_ref, r_old[p]) for p in blocks]
        r_max = jnp.full((1, 1), NEG_BIG, F32)
        for p in blocks:
            acc_sc[p] = acc_old[p] + _dot(ar[p][0].astype(BF16), v_ref[0, pl.ds(start, tk), lanes[p]])
            r_sc[p] = ar[p][1]
            r_max = jnp.maximum(r_max, jnp.max(ar[p][1], axis=0, keepdims=True))
        return (r_max[0, 0] >= SB_DONE).astype(jnp.int32)

    n_full = i * (tq // tk)
    go = jnp.int32(1)
    for d in reversed(range(tq // tk)):
        go = step(n_full + d, True)

    def not_done(c):
        return jnp.logical_and(c[0] >= 0, c[1] > 0)

    def body(c):
        return c[0] - 1, step(c[0], False)

    lax.while_loop(not_done, body, (n_full - 1, go))
    lo_half = _half_masks((tq, LANES))
    for p in range(n_blocks):
        acc = acc_sc[p]
        o = jnp.where(lo_half, acc[0:tq], acc[tq:2 * tq])
        o_ref[0, :, p * LANES:(p + 1) * LANES] = _head_rms(o, seg_ref, g_ref[...]).astype(o_ref.dtype)


def _sb_prompt(q, k, v, g_sb, tri, seg, tq, tk):
    b, s, w = q.shape
    n_blocks = w // LANES
    kern = functools.partial(_sb_prompt_kernel, tq=tq, tk=tk)
    return pl.pallas_call(
        kern,
        grid=(b, s // tq),
        in_specs=[
            pl.BlockSpec((1, tq, w), lambda bi, i: (bi, i, 0)),
            pl.BlockSpec((1,) + k.shape[1:], lambda bi, i: (bi, 0, 0, 0)),
            pl.BlockSpec((1, s, w), lambda bi, i: (bi, 0, 0)),
            pl.BlockSpec((1, LANES), lambda bi, i: (0, 0)),
            pl.BlockSpec(tri.shape, lambda bi, i: (0, 0)),
            pl.BlockSpec(seg.shape, lambda bi, i: (0, 0)),
        ],
        out_specs=pl.BlockSpec((1, tq, w), lambda bi, i: (bi, i, 0)),
        out_shape=jax.ShapeDtypeStruct((b, s, w), BF16),
        scratch_shapes=[
            pltpu.VMEM((n_blocks, 2 * tq, LANES), BF16),
            pltpu.VMEM((n_blocks, 2 * tq, 1), F32),
            pltpu.VMEM((n_blocks, 2 * tq, LANES), F32),
        ],
        compiler_params=_cparams(("arbitrary", "arbitrary")),
        name="sb_prompt",
    )(q, k, v, g_sb, tri, seg)


Q_ROWS = 16


def _page_spec(n_pages, first_of_chunk, r, page_shape):
    def index_map(b, c, pt_ref):
        return (pt_ref[b * n_pages + first_of_chunk(c) + r], 0, 0)
    return pl.BlockSpec((1,) + page_shape, index_map)


def _page_scores(q, k_refs, rows):
    return _dot(q, jnp.concatenate([r[0, rows, :] for r in k_refs], axis=1).astype(BF16))


def _diff_sample_kernel(pt_ref, lam_ref, q_ref, kn_ref, vn_ref, g_ref, *rest, n_pp, t_new, out_scale):
    k_refs, v_refs = rest[:n_pp], rest[n_pp:2 * n_pp]
    o_ref, m_sc, l_sc, acc_sc = rest[2 * n_pp:]
    c = pl.program_id(1)
    n_heads = q_ref.shape[-1] // LANES

    @pl.when(c == 0)
    def _():
        m_sc[...] = jnp.full(m_sc.shape, NEG_BIG, F32)
        l_sc[...] = jnp.zeros(l_sc.shape, F32)
        acc_sc[...] = jnp.zeros(acc_sc.shape, F32)

    def update(h, s, pv):
        m = m_sc[h]
        m_new = jnp.maximum(m, jnp.max(s, axis=-1, keepdims=True))
        alpha = jnp.exp(m - m_new)
        p = jnp.exp(s - m_new)
        l_sc[h] = alpha * l_sc[h] + jnp.sum(p, axis=-1, keepdims=True)
        acc_sc[h] = alpha * acc_sc[h] + pv(p.astype(BF16))
        m_sc[h] = m_new

    page_size = k_refs[0].shape[-1]
    heads = range(n_heads)
    lanes = [slice(h * LANES, (h + 1) * LANES) for h in heads]
    m_old = [m_sc[h] for h in heads]
    l_old = [l_sc[h] for h in heads]
    acc_old = [acc_sc[h] for h in heads]
    s = [_page_scores(q_ref[0, :, lanes[h]], k_refs, lanes[h]) for h in heads]
    m_new = [jnp.maximum(m_old[h], jnp.max(s[h], axis=-1, keepdims=True)) for h in heads]
    p = [jnp.exp(s[h] - m_new[h]) for h in heads]
    alpha = [jnp.exp(m_old[h] - m_new[h]) for h in heads]
    for h in heads:
        vb = jnp.concatenate([v_r[0, pl.ds(h, page_size, stride=n_heads), :] for v_r in v_refs], axis=0)
        pv = _dot(p[h].astype(BF16), vb.astype(BF16))
        m_sc[h] = m_new[h]
        l_sc[h] = alpha[h] * l_old[h] + jnp.sum(p[h], axis=-1, keepdims=True)
        acc_sc[h] = alpha[h] * acc_old[h] + pv

    @pl.when(c == pl.num_programs(1) - 1)
    def _():
        lam = lam_ref[0, 0]
        row = lax.broadcasted_iota(jnp.int32, (Q_ROWS, Q_ROWS), 0)
        col = lax.broadcasted_iota(jnp.int32, (Q_ROWS, Q_ROWS), 1)
        valid = jnp.logical_and(col < t_new, col <= row % t_new)
        for h in range(n_heads):
            ls = slice(h * LANES, (h + 1) * LANES)
            s = jnp.where(valid, _dot_t(q_ref[0, :, ls], kn_ref[0, :, ls]), NEG_BIG)
            update(h, s, lambda p, ls=ls: _dot(p, vn_ref[0, :, ls]))
            o_maps = acc_sc[h] / l_sc[h]
            o = o_maps[0:8] - lam * o_maps[t_new:t_new + 8]
            ms = jnp.mean(o * o, axis=-1, keepdims=True)
            o_ref[0, :, ls] = o * lax.rsqrt(ms + NORM_EPS) * g_ref[...] * out_scale


def _diff_sample(pt_flat, lam, q16, k_new, v_new, g_subln, cache_k, cache_v, n_pages, n_pp, t_new, out_scale):
    nb, _, w = q16.shape
    n_heads = w // LANES
    per_seq = lambda b, c, pt: (b, 0, 0)
    chunk0 = lambda c: c * n_pp
    pages = lambda cache: [_page_spec(n_pages, chunk0, r, cache.shape[1:]) for r in range(n_pp)]
    kern = functools.partial(_diff_sample_kernel, n_pp=n_pp, t_new=t_new, out_scale=out_scale)
    grid_spec = pltpu.PrefetchScalarGridSpec(
        num_scalar_prefetch=1,
        grid=(nb, n_pages // n_pp),
        in_specs=[
            pl.BlockSpec(memory_space=pltpu.SMEM),
            pl.BlockSpec((1, Q_ROWS, w), per_seq),
            pl.BlockSpec((1, Q_ROWS, w), per_seq),
            pl.BlockSpec((1, Q_ROWS, w), per_seq),
            pl.BlockSpec((1, LANES), lambda b, c, pt: (0, 0)),
        ] + pages(cache_k) + pages(cache_v),
        out_specs=pl.BlockSpec((1, 8, w), per_seq),
        scratch_shapes=[
            pltpu.VMEM((n_heads, Q_ROWS, 1), F32),
            pltpu.VMEM((n_heads, Q_ROWS, 1), F32),
            pltpu.VMEM((n_heads, Q_ROWS, LANES), F32),
        ],
    )
    return pl.pallas_call(
        kern,
        grid_spec=grid_spec,
        out_shape=jax.ShapeDtypeStruct((nb, 8, w), F32),
        compiler_params=_cparams(("arbitrary", "arbitrary")),
        name="diff_sample",
    )(pt_flat, lam, q16, k_new, v_new, g_subln, *([cache_k] * n_pp), *([cache_v] * n_pp))


def _sb_sample_kernel(pt_ref, q_ref, kn_ref, vn_ref, g_ref, tri_ref, trin_ref, seg_ref, acc_in, r_in, *rest,
                      n_pp, t_new, with_new):
    k_refs, v_refs = rest[:n_pp], rest[n_pp:2 * n_pp]
    o_ref, acc_ref, r_ref = rest[2 * n_pp:]
    c = pl.program_id(1)
    n_blocks = q_ref.shape[-1] // LANES

    blocks = range(n_blocks)
    lanes = [slice(p * LANES, (p + 1) * LANES) for p in blocks]
    if with_new:
        row = lax.broadcasted_iota(jnp.int32, (Q_ROWS, Q_ROWS), 0)
        col = lax.broadcasted_iota(jnp.int32, (Q_ROWS, Q_ROWS), 1)
        valid = jnp.logical_and(col < t_new, col < row % t_new)
        r_old, acc_old = [], []
        for p in blocks:
            z_new = _dot_t(q_ref[0, :, lanes[p]], kn_ref[0, :, lanes[p]])
            a, r = _sb_block(z_new, valid, trin_ref, jnp.zeros((Q_ROWS, 1), F32))
            r_old.append(r)
            acc_old.append(_dot(a.astype(BF16), vn_ref[0, :, lanes[p]]))
    else:
        @pl.when(c == 0)
        def _():
            acc_ref[...] = acc_in[...]
            r_ref[...] = r_in[...]

        r_old = [r_ref[0, :, p * LANES:p * LANES + 1] for p in blocks]
        acc_old = [acc_ref[0, :, lanes[p]] for p in blocks]

    z = [_page_scores(q_ref[0, :, lanes[p]], k_refs, lanes[p]) for p in blocks]
    ar = [_sb_block(z[p], None, tri_ref, r_old[p]) for p in blocks]
    acc_new = []
    for p in blocks:
        vt = jnp.concatenate([v_r[0, lanes[p], :] for v_r in v_refs], axis=1).astype(BF16)
        acc_new.append(acc_old[p] + _dot_t(ar[p][0].astype(BF16), vt))
        acc_ref[0, :, lanes[p]] = acc_new[p]
        r_ref[0, :, lanes[p]] = jnp.broadcast_to(ar[p][1], (Q_ROWS, LANES))

    def finalize(accs):
        lo_half = _half_masks((8, LANES))
        for p in blocks:
            o = jnp.where(lo_half, accs[p][0:8], accs[p][t_new:t_new + 8])
            o_ref[0, :, lanes[p]] = _head_rms(o, seg_ref, g_ref[...])

    if with_new:
        finalize(acc_new)
    else:
        @pl.when(c == pl.num_programs(1) - 1)
        def _():
            finalize([acc_ref[0, :, lanes[p]] for p in blocks])


def _sb_sample(pt_flat, q16, k_new, v_new, g_sb, tri, tri_new, seg, acc_in, r_in, cache_k, cache_v,
               n_pages, page_hi, n_chunks, n_pp, t_new, with_new):
    assert not with_new or n_chunks == 1
    nb, _, w = q16.shape
    per_seq = lambda b, c, pt: (b, 0, 0)
    fixed = lambda b, c, pt: (0, 0)
    first = lambda c: page_hi - (c + 1) * n_pp
    pages = lambda cache: [_page_spec(n_pages, first, r, cache.shape[1:]) for r in range(n_pp)]
    kern = functools.partial(_sb_sample_kernel, n_pp=n_pp, t_new=t_new, with_new=with_new)
    state_spec = pl.BlockSpec((1, Q_ROWS, w), per_seq)
    grid_spec = pltpu.PrefetchScalarGridSpec(
        num_scalar_prefetch=1,
        grid=(nb, n_chunks),
        in_specs=[
            state_spec, state_spec, state_spec,
            pl.BlockSpec((1, LANES), fixed),
            pl.BlockSpec(tri.shape, fixed),
            pl.BlockSpec(tri_new.shape, fixed),
            pl.BlockSpec(seg.shape, fixed),
            state_spec, state_spec,
        ] + pages(cache_k) + pages(cache_v),
        out_specs=[pl.BlockSpec((1, 8, w), per_seq), state_spec, state_spec],
    )
    state = jax.ShapeDtypeStruct((nb, Q_ROWS, w), F32)
    return pl.pallas_call(
        kern,
        grid_spec=grid_spec,
        out_shape=[jax.ShapeDtypeStruct((nb, 8, w), F32), state, state],
        compiler_params=_cparams(("arbitrary", "arbitrary")),
        name="sb_sample_new" if with_new else "sb_sample_rest",
    )(pt_flat, q16, k_new, v_new, g_sb, tri, tri_new, seg, acc_in, r_in,
      *([cache_k] * n_pp), *([cache_v] * n_pp))


def _pack_bf16_pairs(x):
    c = x.shape[-1] // 2
    bits = lax.bitcast_convert_type(x.astype(F32), jnp.uint32)
    return lax.bitcast_convert_type((bits[:, :c] >> 16) | (bits[:, c:] & jnp.uint32(0xFFFF0000)), F32)


def _unpack_bf16_pairs(words):
    w = lax.bitcast_convert_type(words, jnp.uint32)
    lo = lax.bitcast_convert_type(w << 16, F32)
    hi = lax.bitcast_convert_type(w & jnp.uint32(0xFFFF0000), F32)
    return jnp.concatenate([lo, hi], axis=1).astype(BF16)


def _finish_kernel(x_ref, od_ref, os_ref, wo_ref, g_ref, wr_hi_ref, wr_lo_ref, br_ref,
                   h_ref, hn_ref, route_ref):
    half = od_ref.shape[-1]
    h = x_ref[...] + _dot(od_ref[...], wo_ref[0:half, :]) + _dot(os_ref[...], wo_ref[half:, :])
    h_ref[...] = h
    ms = jnp.mean(h * h, axis=-1, keepdims=True)
    hn = h * lax.rsqrt(ms + NORM_EPS) * g_ref[...]
    hn_hi, hn_lo = _split_bf16(hn)
    hn_ref[...] = _pack_bf16_pairs(hn_hi)

    logits = (_dot(hn_hi, wr_hi_ref[...]) + _dot(hn_lo, wr_hi_ref[...]) + _dot(hn_hi, wr_lo_ref[...])
              + br_ref[...])
    n_exp = N_GROUPS * EXPERTS_PER_GROUP
    lane = lax.broadcasted_iota(jnp.int32, logits.shape, 1)

    def first_argmax(v, vmax):
        return jnp.min(jnp.where(v == vmax, lane, LANES), axis=-1, keepdims=True)

    lg = jnp.where(jnp.logical_and(lane >= n_exp, lane < n_exp + N_GROUPS), logits, NEG_BIG)
    mg = jnp.max(lg, axis=-1, keepdims=True)
    grp = first_argmax(lg, mg) - n_exp
    p_top = 1.0 / jnp.sum(jnp.exp(lg - mg), axis=-1, keepdims=True)

    le = jnp.where(jnp.logical_and(lane < n_exp, lane // EXPERTS_PER_GROUP == grp), logits, NEG_BIG)
    m1 = jnp.max(le, axis=-1, keepdims=True)
    i1 = first_argmax(le, m1)
    le2 = jnp.where(lane == i1, NEG_BIG, le)
    m2 = jnp.max(le2, axis=-1, keepdims=True)
    i2 = first_argmax(le2, m2)
    e2 = jnp.exp(m2 - m1)
    gate1 = p_top / (1.0 + e2)
    gate2 = p_top * e2 / (1.0 + e2)
    route = jnp.where(lane == 0, i1.astype(F32),
                      jnp.where(lane == 1, i2.astype(F32),
                                jnp.where(lane == 2, gate1, jnp.where(lane == 3, gate2, 0.0))))
    route_ref[...] = route


def _finish(x2d, od, osb, w_o_bf, g_ffn, wr_hi, wr_lo, b_r, tm):
    n, d = x2d.shape
    half = od.shape[-1]
    row = lambda i: (i, 0)
    fixed = lambda i: (0, 0)
    return pl.pallas_call(
        _finish_kernel,
        grid=(n // tm,),
        in_specs=[
            pl.BlockSpec((tm, d), row),
            pl.BlockSpec((tm, half), row),
            pl.BlockSpec((tm, half), row),
            pl.BlockSpec(w_o_bf.shape, fixed),
            pl.BlockSpec((1, d), fixed),
            pl.BlockSpec(wr_hi.shape, fixed),
            pl.BlockSpec(wr_lo.shape, fixed),
            pl.BlockSpec((1, LANES), fixed),
        ],
        out_specs=[pl.BlockSpec((tm, d), row), pl.BlockSpec((tm, d // 2), row), pl.BlockSpec((tm, LANES), row)],
        out_shape=[jax.ShapeDtypeStruct((n, d), F32), jax.ShapeDtypeStruct((n, d // 2), F32),
                   jax.ShapeDtypeStruct((n, LANES), F32)],
        compiler_params=_cparams(("arbitrary",)),
        name="out_proj_router",
    )(x2d, od, osb, w_o_bf, g_ffn, wr_hi, wr_lo, b_r)


def _expert_kernel(be_ref, nact_ref, x_ref, wg_ref, wu_ref, wd_ref, y_ref, wg_sc, wu_sc, wd_sc):
    blk = pl.program_id(0)
    active = blk < nact_ref[0]

    @pl.when(jnp.logical_and(active, jnp.logical_or(blk == 0, be_ref[blk] != be_ref[jnp.maximum(blk - 1, 0)])))
    def _():
        wg_sc[...] = wg_ref[0].astype(BF16)
        wu_sc[...] = wu_ref[0].astype(BF16)
        wd_sc[...] = wd_ref[0].astype(BF16)

    @pl.when(active)
    def _():
        x = _unpack_bf16_pairs(x_ref[...])
        gate = _dot(x, wg_sc[...])
        up = _dot(x, wu_sc[...])
        hidden = (gate * jax.nn.sigmoid(gate) * up).astype(BF16)
        y_ref[...] = _dot(hidden, wd_sc[...])

    @pl.when(blk >= nact_ref[0])
    def _():
        y_ref[...] = jnp.zeros(y_ref.shape, y_ref.dtype)


def _expert_ffn(block_exp, n_active, xs, w_gate, w_up, w_down, bm):
    n_slots = xs.shape[0]
    d, ff = w_gate.shape[-2:]
    grid_spec = pltpu.PrefetchScalarGridSpec(
        num_scalar_prefetch=2,
        grid=(n_slots // bm,),
        in_specs=[
            pl.BlockSpec((bm, xs.shape[1]), lambda i, be, na: (i, 0)),
            pl.BlockSpec((1, d, ff), lambda i, be, na: (be[i], 0, 0)),
            pl.BlockSpec((1, d, ff), lambda i, be, na: (be[i], 0, 0)),
            pl.BlockSpec((1, ff, d), lambda i, be, na: (be[i], 0, 0)),
        ],
        out_specs=pl.BlockSpec((bm, d), lambda i, be, na: (i, 0)),
        scratch_shapes=[pltpu.VMEM((d, ff), BF16), pltpu.VMEM((d, ff), BF16), pltpu.VMEM((ff, d), BF16)],
    )
    return pl.pallas_call(
        _expert_kernel,
        grid_spec=grid_spec,
        out_shape=jax.ShapeDtypeStruct((n_slots, d), F32),
        compiler_params=_cparams(("arbitrary",)),
        name="expert_ffn",
    )(block_exp, n_active, xs, w_gate, w_up, w_down)


def _moe(h_parts, hn_packed, route, w_gate, w_up, w_down, bm):
    n_tok = hn_packed.shape[0]
    n_exp = w_gate.shape[0]
    ids = route[:, 0:2].astype(jnp.int32)
    gates = route[:, 2:4]
    n_assign = 2 * n_tok
    flat_e = ids.reshape(-1)
    experts = jnp.arange(n_exp, dtype=jnp.int32)
    order = jnp.argsort(flat_e, stable=True).astype(jnp.int32)
    rank = jnp.argsort(order).astype(jnp.int32)
    counts = jnp.sum((flat_e[:, None] == experts[None, :]).astype(jnp.int32), axis=0)
    padded = (counts + bm - 1) // bm * bm
    start = jnp.cumsum(counts) - counts
    pend = jnp.cumsum(padded)
    pstart = pend - padded
    dest = (pstart[flat_e] + rank - start[flat_e]).reshape(n_tok, 2)
    n_blocks = -(-n_assign // bm) + n_exp
    blk_start = jnp.arange(n_blocks, dtype=jnp.int32) * bm
    block_exp = jnp.minimum(jnp.sum((blk_start[:, None] >= pend[None, :]).astype(jnp.int32), axis=1), n_exp - 1)
    n_active = (pend[-1:] // bm).astype(jnp.int32)
    offset = (blk_start - pstart[block_exp])[:, None] + jnp.arange(bm, dtype=jnp.int32)[None, :]
    src = jnp.minimum(start[block_exp][:, None] + offset, n_assign - 1)
    slot_tok = jnp.where(offset < counts[block_exp][:, None], order[src] // 2, 0).reshape(-1)
    xs = hn_packed[slot_tok]
    yb = _expert_ffn(block_exp, n_active, xs, w_gate, w_up, w_down, bm)
    outs, lo = [], 0
    for h in h_parts:
        hi = lo + h.shape[0]
        outs.append(h + (gates[lo:hi, 0:1] * yb[dest[lo:hi, 0]] + gates[lo:hi, 1:2] * yb[dest[lo:hi, 1]]))
        lo = hi
    return outs


def _block_diag_mean(width, group):
    idx = jnp.arange(width) // group
    return jnp.where(idx[:, None] == idx[None, :], 1.0 / group, 0.0).astype(BF16)


def _tri(n):
    idx = jnp.arange(n)
    tri = (idx[:, None] > idx[None, :]).astype(BF16)
    return jnp.concatenate([tri, tri], axis=0)


def _sample_rows(a, nb, t):
    w = a.shape[-1]
    a = a.reshape(nb, t, w)
    lo = (jnp.arange(w) % LANES) < DH
    zeros = jnp.zeros((nb, Q_ROWS - 2 * t, w), a.dtype)
    return jnp.concatenate([jnp.where(lo, a, 0), jnp.where(lo, 0, a), zeros], axis=1)


def _pad_rows(a, nb, t):
    w = a.shape[-1]
    a = a.reshape(nb, t, w)
    return jnp.concatenate([a, jnp.zeros((nb, Q_ROWS - t, w), a.dtype)], axis=1)


def kernel(x_prompt, x_sample, cache_k_diff, cache_v_diff, cache_k_sb, cache_v_sb, page_table, g_attn_norm, w_in, g_q_norm, g_k_norm, lambda_q1, lambda_k1, lambda_q2, lambda_k2, g_subln, g_sb_out, w_o, g_ffn_norm, w_router_group, b_router_group, w_router_expert, b_router_expert, w_exp_gate, w_exp_up, w_exp_down):
    depth = w_in.shape[0]
    b, s, d = x_prompt.shape
    nb, t_new, _ = x_sample.shape
    n_phys, page_size = cache_k_diff.shape[1], cache_k_diff.shape[2]
    n_pages = page_table.shape[1]
    past_len = n_pages * page_size
    width = w_in.shape[-1] // 6
    n_exp = w_exp_gate.shape[1]

    tm = min(256, nb * t_new)
    tq, tk = min(PROMPT_TQ, s), min(PROMPT_TK, s)
    n_pp_diff = min(16, n_pages)
    n_pp_sb = min(2, n_pages)
    bm = 256

    seg64 = _block_diag_mean(width, DH)
    seg64_lane = _block_diag_mean(LANES, DH)
    tri_p = _tri(tk)
    tri_s = _tri(n_pp_sb * page_size)
    tri_new = _tri(Q_ROWS)
    pos_p = _rope_tables(jnp.arange(s), 1)
    pos_s = _rope_tables(past_len + jnp.arange(t_new), tm // t_new)

    hp = x_prompt.reshape(b * s, d)
    hs = x_sample.reshape(nb * t_new, d)
    outs = [[] for _ in range(8)]
    for layer in range(depth):
        lam_init = 0.8 - 0.6 * math.exp(-0.3 * layer)
        lam = (jnp.exp(jnp.sum(lambda_q1[layer].astype(F32) * lambda_k1[layer].astype(F32)))
               - jnp.exp(jnp.sum(lambda_q2[layer].astype(F32) * lambda_k2[layer].astype(F32))) + lam_init)
        lam = lam.reshape(1, 1).astype(F32)
        out_scale = 1.0 - lam_init
        g_attn = g_attn_norm[layer].reshape(1, d)
        w_in_bf = w_in[layer].astype(BF16)
        gq = jnp.tile(g_q_norm[layer], width // DH).reshape(1, width)
        gk = jnp.tile(g_k_norm[layer], width // DH).reshape(1, width)
        g_sub = g_subln[layer].reshape(1, LANES)
        g_sb = jnp.tile(g_sb_out[layer], LANES // DH).reshape(1, LANES)

        (kd_p, vd_p, ks_p, vs_p, qd_pb, kd_pb, vd_pb, qs_pb, ks_pb, vs_pb) = _project(
            hp, pos_p, g_attn, w_in_bf, gq, gk, seg64, tk, seq_len=s)
        (kd_s, vd_s, ks_s, vs_s, qd_sb, kd_sb, vd_sb, qs_sb, ks_sb, vs_sb) = _project(
            hs, pos_s, g_attn, w_in_bf, gq, gk, seg64, tm)

        r3 = lambda a: a.reshape(b, s, width)
        qd3, kd3, vd3 = r3(qd_pb), kd_pb, r3(vd_pb)
        qmax = _normed_head_bound(g_q_norm[layer]) * (DH ** -0.5)
        kmax = _normed_head_bound(g_k_norm[layer])
        od_p = lax.cond(
            qmax * kmax <= BOUND_MAX,
            lambda: _diff_prompt_bounded(lam, kmax.reshape(1, 1), qd3, kd3, vd3, g_sub, out_scale, tq, tk),
            lambda: _diff_prompt(lam, qd3, kd3, vd3, g_sub, out_scale, tq, tk))
        osb_p = _sb_prompt(r3(qs_pb), ks_pb, r3(vs_pb), g_sb, tri_p, seg64_lane, tq, tk)

        pt_flat = (page_table.astype(jnp.int32) + layer * n_phys).reshape(-1)
        transposed_pages = lambda c: jnp.transpose(c, (0, 1, 3, 4, 2)).reshape(depth * n_phys, width, page_size)
        v_diff_pages = cache_v_diff.reshape(depth * n_phys, page_size * (width // DV_DIFF_W), DV_DIFF_W)
        od_s = _diff_sample(pt_flat, lam, _sample_rows(qd_sb, nb, t_new), _pad_rows(kd_sb, nb, t_new),
                            _pad_rows(vd_sb, nb, t_new), g_sub, transposed_pages(cache_k_diff),
                            v_diff_pages, n_pages, n_pp_diff, t_new, out_scale)
        q16s = _sample_rows(qs_sb, nb, t_new)
        kn16, vn16 = _pad_rows(ks_sb, nb, t_new), _pad_rows(vs_sb, nb, t_new)
        cks, cvs = transposed_pages(cache_k_sb), transposed_pages(cache_v_sb)
        zeros_state = jnp.zeros((nb, Q_ROWS, width), F32)
        sb_args = (pt_flat, q16s, kn16, vn16, g_sb, tri_s, tri_new, seg64_lane)
        osb_s, acc_s, r_s = _sb_sample(*sb_args, zeros_state, zeros_state, cks, cvs,
                                       n_pages, n_pages, 1, n_pp_sb, t_new, True)
        n_rest = (n_pages - n_pp_sb) // n_pp_sb
        if n_rest > 0:
            unfinished = jnp.max(r_s[:, :2 * t_new, :]) >= SB_DONE
            osb_s = lax.cond(
                unfinished,
                lambda: _sb_sample(*sb_args, acc_s, r_s, cks, cvs, n_pages, n_pages - n_pp_sb,
                                   n_rest, n_pp_sb, t_new, False)[0],
                lambda: osb_s)

        w_o_bf = w_o[layer].astype(BF16)
        g_ffn = g_ffn_norm[layer].reshape(1, d)
        w_r = jnp.zeros((d, LANES), F32)
        w_r = w_r.at[:, :n_exp].set(w_router_expert[layer]).at[:, n_exp:n_exp + N_GROUPS].set(w_router_group[layer])
        wr_hi, wr_lo = _split_bf16(w_r)
        b_r = jnp.zeros((1, LANES), F32)
        b_r = b_r.at[0, :n_exp].set(b_router_expert[layer]).at[0, n_exp:n_exp + N_GROUPS].set(b_router_group[layer])

        od_s2 = od_s[:, :t_new].reshape(nb * t_new, width).astype(BF16)
        osb_s2 = osb_s[:, :t_new].reshape(nb * t_new, width).astype(BF16)
        fin = lambda xx, od, osb, tile: _finish(xx, od, osb, w_o_bf, g_ffn, wr_hi, wr_lo, b_r, tile)
        h_p, hn_p, route_p = fin(hp, od_p.reshape(b * s, width), osb_p.reshape(b * s, width), min(256, s))
        h_s, hn_s, route_s = fin(hs, od_s2, osb_s2, tm)

        hp, hs = _moe((h_p, h_s), jnp.concatenate([hn_p, hn_s]), jnp.concatenate([route_p, route_s]),
                      w_exp_gate[layer], w_exp_up[layer], w_exp_down[layer], bm)

        heads_last = lambda a: jnp.transpose(a.reshape(b, width // DH, DH, s), (0, 3, 1, 2))
        for lst, val in zip(outs, (heads_last(kd_p), vd_p, heads_last(ks_p), heads_last(vs_p),
                                   kd_s, vd_s, ks_s, vs_s)):
            lst.append(val)

    h_diff, h_sb = width // DV_DIFF_W, width // DH
    shapes_p = [(b, s, 2 * h_diff, DH), (b, s, h_diff, DV_DIFF_W), (b, s, h_sb, DH), (b, s, h_sb, DH)]
    shapes_s = [(nb, t_new) + sh[2:] for sh in shapes_p]
    stacked = [jnp.stack([v.reshape(sh) for v in lst]) for lst, sh in zip(outs, shapes_p + shapes_s)]
    return (hp.reshape(b, s, d), hs.reshape(nb, t_new, d), *stacked)
```

```python
import functools
import math

import jax
import jax.numpy as jnp
from jax import lax
from jax.experimental import pallas as pl
from jax.experimental.pallas import tpu as pltpu

F32 = jnp.float32
BF16 = jnp.bfloat16

NORM_EPS = 1e-6
ROPE_THETA = 10000.0
DH = 64
DV_DIFF_W = 2 * DH
LANES = 128
N_GROUPS = 4
EXPERTS_PER_GROUP = 8
NEG_BIG = -1e30
SB_DONE = -110.0
VMEM_LIMIT = 56 * 1024 * 1024
BOUND_MAX = 40.0
BOUND_SLACK = 1.01
PROMPT_TQ = 256
PROMPT_TK = 256


def _cparams(sem):
    return pltpu.CompilerParams(dimension_semantics=sem, vmem_limit_bytes=VMEM_LIMIT)


def _dot_t(a, b):
    return lax.dot_general(a, b, (((1,), (1,)), ((), ())), preferred_element_type=F32)


def _dot(a, b):
    return jnp.dot(a, b, preferred_element_type=F32)


def _split_bf16(x):
    hi = x.astype(BF16)
    lo = (x - hi.astype(F32)).astype(BF16)
    return hi, lo


def _proj_kernel(x_ref, g_ref, w_ref, gq_ref, gk_ref, cos_ref, sin_ref, seg_ref,
                 kd_ref, vd_ref, ks_ref, vs_ref,
                 qd_bf, kd_bf, vd_bf, qs_bf, ks_bf, vs_bf, *, transposed):
    x = x_ref[...]
    ms = jnp.mean(x * x, axis=-1, keepdims=True)
    xn = (x * lax.rsqrt(ms + NORM_EPS) * g_ref[...]).astype(BF16)
    width = vd_bf.shape[-1]

    def put_key_like(f32_ref, bf_ref, val):
        if transposed:
            val_t = val.T
            f32_ref[0] = val_t
            if bf_ref is not None:
                bf_ref[0, 0] = val_t.astype(BF16)
        else:
            f32_ref[...] = val
            if bf_ref is not None:
                bf_ref[...] = val.astype(BF16)

    def seg(c):
        return _dot(xn, w_ref[:, c * width:(c + 1) * width])

    reps = width // LANES
    cos = jnp.concatenate([cos_ref[...]] * reps, axis=1)
    sin = jnp.concatenate([sin_ref[...]] * reps, axis=1)
    lane = lax.broadcasted_iota(jnp.int32, (x.shape[0], width), 1)
    first_half = (lane % DH) < (DH // 2)

    def norm_rope(p, g):
        ms_h = _dot((p * p).astype(BF16), seg_ref[...])
        y = p * lax.rsqrt(ms_h + NORM_EPS) * g
        partner = jnp.where(first_half, pltpu.roll(y, width - DH // 2, 1), pltpu.roll(y, DH // 2, 1))
        return y * cos + partner * sin

    scale = DH ** -0.5
    qd = norm_rope(seg(0), gq_ref[...])
    qd_bf[...] = (qd * scale).astype(BF16)
    put_key_like(kd_ref, kd_bf, norm_rope(seg(1), gk_ref[...]))
    vd = seg(2)
    if transposed:
        n_vh = width // DV_DIFF_W
        for h in range(n_vh):
            vd_ref[pl.ds(h, x.shape[0], stride=n_vh), :] = vd[:, h * DV_DIFF_W:(h + 1) * DV_DIFF_W]
    else:
        vd_ref[...] = vd
    vd_bf[...] = vd.astype(BF16)
    qs_bf[...] = (seg(3) * scale).astype(BF16)
    put_key_like(ks_ref, ks_bf, seg(4))
    vs = seg(5)
    put_key_like(vs_ref, None, vs)
    vs_bf[...] = vs.astype(BF16)


def _rope_tables(pos, reps):
    half = DH // 2
    inv = jnp.power(jnp.float32(ROPE_THETA), -(jnp.arange(half, dtype=F32) * 2.0 / DH))
    ang = pos.astype(F32)[:, None] * inv[None, :]
    cos, sin = jnp.cos(ang), jnp.sin(ang)
    cos_t = jnp.concatenate([cos, cos] * (LANES // DH), axis=1)
    sin_t = jnp.concatenate([-sin, sin] * (LANES // DH), axis=1)
    return jnp.tile(cos_t, (reps, 1)), jnp.tile(sin_t, (reps, 1))


def _project(x2d, pos_tables, g_attn, w_in_bf, gq, gk, seg_mat, tm, seq_len=None):
    n, d = x2d.shape
    width = w_in_bf.shape[1] // 6
    cos_t, sin_t = pos_tables
    n_pos_blocks = cos_t.shape[0] // tm
    row = lambda i: (i, 0)
    fixed = lambda i: (0, 0)
    out_f32 = jax.ShapeDtypeStruct((n, width), F32)
    out_bf = jax.ShapeDtypeStruct((n, width), BF16)
    natural = pl.BlockSpec((tm, width), row)
    out_specs = [natural] * 10
    out_shape = [out_f32] * 4 + [out_bf] * 6
    if seq_len is not None:
        n_sb = seq_len // tm
        key_f32 = jax.ShapeDtypeStruct((n // seq_len, width, seq_len), F32)
        key_bf = jax.ShapeDtypeStruct((n // seq_len, n_sb, width, tm), BF16)
        spec_f32 = pl.BlockSpec((1, width, tm), lambda i: (i // n_sb, 0, i % n_sb))
        spec_bf = pl.BlockSpec((1, 1, width, tm), lambda i: (i // n_sb, i % n_sb, 0, 0))
        n_vh = width // DV_DIFF_W
        val_f32 = jax.ShapeDtypeStruct((n * n_vh, DV_DIFF_W), F32)
        spec_val = pl.BlockSpec((tm * n_vh, DV_DIFF_W), row)
        out_specs = [spec_f32, spec_val, spec_f32, spec_f32, natural, spec_bf, natural, natural, spec_bf, natural]
        out_shape = [key_f32, val_f32, key_f32, key_f32, out_bf, key_bf, out_bf, out_bf, key_bf, out_bf]
    return pl.pallas_call(
        functools.partial(_proj_kernel, transposed=seq_len is not None),
        grid=(n // tm,),
        in_specs=[
            pl.BlockSpec((tm, d), row),
            pl.BlockSpec((1, d), fixed),
            pl.BlockSpec(w_in_bf.shape, fixed),
            pl.BlockSpec((1, width), fixed),
            pl.BlockSpec((1, width), fixed),
            pl.BlockSpec((tm, LANES), lambda i: (i % n_pos_blocks, 0)),
            pl.BlockSpec((tm, LANES), lambda i: (i % n_pos_blocks, 0)),
            pl.BlockSpec(seg_mat.shape, fixed),
        ],
        out_specs=out_specs,
        out_shape=out_shape,
        compiler_params=_cparams(("arbitrary",)),
        name="in_proj",
    )(x2d, g_attn, w_in_bf, gq, gk, cos_t, sin_t, seg_mat)


def _half_masks(shape):
    lane = lax.broadcasted_iota(jnp.int32, shape, 1)
    return lane < DH


def _stack_halves(q_ref, q_sc, tq):
    lo_half = _half_masks((tq, LANES))
    zero = jnp.zeros((), BF16)
    for h in range(q_sc.shape[0]):
        qh = q_ref[0, :, h * LANES:(h + 1) * LANES]
        q_sc[h, 0:tq, :] = jnp.where(lo_half, qh, zero)
        q_sc[h, tq:2 * tq, :] = jnp.where(lo_half, zero, qh)


def _diff_prompt_kernel(lam_ref, q_ref, k_ref, v_ref, g_ref, o_ref, q_sc, m_sc, l_sc, acc_sc,
                        *, tq, tk, out_scale):
    i = pl.program_id(1)
    n_heads = q_sc.shape[0]
    _stack_halves(q_ref, q_sc, tq)
    m_sc[...] = jnp.full(m_sc.shape, NEG_BIG, F32)
    l_sc[...] = jnp.zeros(l_sc.shape, F32)
    acc_sc[...] = jnp.zeros(acc_sc.shape, F32)

    def step(j, masked):
        start = pl.multiple_of(j * tk, tk)
        for h in range(n_heads):
            ls = slice(h * LANES, (h + 1) * LANES)
            kb = k_ref[0, j, ls, :]
            vb = v_ref[0, pl.ds(start, tk), ls]
            s = _dot(q_sc[h], kb)
            if masked:
                row = lax.broadcasted_iota(jnp.int32, s.shape, 0) % tq
                col = lax.broadcasted_iota(jnp.int32, s.shape, 1)
                s = jnp.where(j * tk + col <= i * tq + row, s, NEG_BIG)
            m = m_sc[h]
            m_new = jnp.maximum(m, jnp.max(s, axis=-1, keepdims=True))
            alpha = jnp.exp(m - m_new)
            p = jnp.exp(s - m_new)
            l_sc[h] = alpha * l_sc[h] + jnp.sum(p, axis=-1, keepdims=True)
            acc_sc[h] = alpha * acc_sc[h] + _dot(p.astype(BF16), vb)
            m_sc[h] = m_new

    n_full = i * (tq // tk)

    @pl.loop(0, n_full)
    def _(j):
        step(j, False)

    for d in range(tq // tk):
        step(n_full + d, True)

    lam = lam_ref[0, 0]
    for h in range(n_heads):
        o_maps = acc_sc[h] / l_sc[h]
        o = o_maps[0:tq] - lam * o_maps[tq:2 * tq]
        ms = jnp.mean(o * o, axis=-1, keepdims=True)
        o_ref[0, :, h * LANES:(h + 1) * LANES] = (
            o * lax.rsqrt(ms + NORM_EPS) * g_ref[...] * out_scale).astype(o_ref.dtype)


def _diff_prompt_bounded_kernel(lam_ref, kmax_ref, q_ref, k_ref, v_ref, g_ref, o_ref, q_sc, c_sc, acc_sc,
                                *, tq, tk, out_scale):
    i = pl.program_id(1)
    n_heads = q_sc.shape[0]
    _stack_halves(q_ref, q_sc, tq)
    for h in range(n_heads):
        qf = q_sc[h].astype(F32)
        c_sc[h] = jnp.sqrt(jnp.sum(qf * qf, axis=-1, keepdims=True)) * kmax_ref[0, 0]
    acc_sc[...] = jnp.zeros(acc_sc.shape, F32)
    ones = jnp.ones((tk, LANES), BF16)

    def step(j, masked):
        start = pl.multiple_of(j * tk, tk)
        heads = range(n_heads)
        lanes = [slice(h * LANES, (h + 1) * LANES) for h in heads]
        acc_old = [acc_sc[h] for h in heads]
        s = [_dot(q_sc[h], k_ref[0, j, lanes[h], :]) for h in heads]
        if masked:
            row = lax.broadcasted_iota(jnp.int32, s[0].shape, 0) % tq
            col = lax.broadcasted_iota(jnp.int32, s[0].shape, 1)
            s = [jnp.where(j * tk + col <= i * tq + row, s[h], NEG_BIG) for h in heads]
        p = [jnp.exp(s[h] - c_sc[h]).astype(BF16) for h in heads]
        for h in heads:
            vb = v_ref[0, pl.ds(start, tk), lanes[h]]
            acc_sc[h] = acc_old[h] + _dot(p[h], jnp.concatenate([vb, ones], axis=1))

    n_full = i * (tq // tk)

    @pl.loop(0, n_full)
    def _(j):
        step(j, False)

    for d in range(tq // tk):
        step(n_full + d, True)

    lam = lam_ref[0, 0]
    for h in range(n_heads):
        acc = acc_sc[h]
        o_maps = acc[:, 0:LANES] / acc[:, LANES:2 * LANES]
        o = o_maps[0:tq] - lam * o_maps[tq:2 * tq]
        ms = jnp.mean(o * o, axis=-1, keepdims=True)
        o_ref[0, :, h * LANES:(h + 1) * LANES] = (
            o * lax.rsqrt(ms + NORM_EPS) * g_ref[...] * out_scale).astype(o_ref.dtype)


def _diff_prompt_bounded(lam, kmax, q, k, v, g_subln, out_scale, tq, tk):
    b, s, w = q.shape
    n_heads = w // LANES
    kern = functools.partial(_diff_prompt_bounded_kernel, tq=tq, tk=tk, out_scale=out_scale)
    return pl.pallas_call(
        kern,
        grid=(b, s // tq),
        in_specs=[
            pl.BlockSpec(memory_space=pltpu.SMEM),
            pl.BlockSpec(memory_space=pltpu.SMEM),
            pl.BlockSpec((1, tq, w), lambda bi, i: (bi, i, 0)),
            pl.BlockSpec((1,) + k.shape[1:], lambda bi, i: (bi, 0, 0, 0)),
            pl.BlockSpec((1, s, w), lambda bi, i: (bi, 0, 0)),
            pl.BlockSpec((1, LANES), lambda bi, i: (0, 0)),
        ],
        out_specs=pl.BlockSpec((1, tq, w), lambda bi, i: (bi, i, 0)),
        out_shape=jax.ShapeDtypeStruct((b, s, w), BF16),
        scratch_shapes=[
            pltpu.VMEM((n_heads, 2 * tq, LANES), BF16),
            pltpu.VMEM((n_heads, 2 * tq, 1), F32),
            pltpu.VMEM((n_heads, 2 * tq, 2 * LANES), F32),
        ],
        compiler_params=_cparams(("arbitrary", "arbitrary")),
        name="diff_prompt_bounded",
    )(lam, kmax, q, k, v, g_subln)


def _normed_head_bound(gain):
    return jnp.max(jnp.abs(gain.astype(F32))) * (DH ** 0.5) * BOUND_SLACK


def _diff_prompt(lam, q, k, v, g_subln, out_scale, tq, tk):
    b, s, w = q.shape
    n_heads = w // LANES
    kern = functools.partial(_diff_prompt_kernel, tq=tq, tk=tk, out_scale=out_scale)
    return pl.pallas_call(
        kern,
        grid=(b, s // tq),
        in_specs=[
            pl.BlockSpec(memory_space=pltpu.SMEM),
            pl.BlockSpec((1, tq, w), lambda bi, i: (bi, i, 0)),
            pl.BlockSpec((1,) + k.shape[1:], lambda bi, i: (bi, 0, 0, 0)),
            pl.BlockSpec((1, s, w), lambda bi, i: (bi, 0, 0)),
            pl.BlockSpec((1, LANES), lambda bi, i: (0, 0)),
        ],
        out_specs=pl.BlockSpec((1, tq, w), lambda bi, i: (bi, i, 0)),
        out_shape=jax.ShapeDtypeStruct((b, s, w), BF16),
        scratch_shapes=[
            pltpu.VMEM((n_heads, 2 * tq, LANES), BF16),
            pltpu.VMEM((n_heads, 2 * tq, 1), F32),
            pltpu.VMEM((n_heads, 2 * tq, 1), F32),
            pltpu.VMEM((n_heads, 2 * tq, LANES), F32),
        ],
        compiler_params=_cparams(("arbitrary", "arbitrary")),
        name="diff_prompt",
    )(lam, q, k, v, g_subln)


def _sb_block(z, valid, tri_ref, r_in):
    log_1m = jnp.minimum(-z, 0.0) - jnp.log(1.0 + jnp.exp(-jnp.abs(z)))
    if valid is not None:
        log_1m = jnp.where(valid, log_1m, 0.0)
    later = _dot(jnp.concatenate(_split_bf16(log_1m), axis=1), tri_ref[...])
    a = jnp.exp((z + log_1m) + (later + r_in))
    if valid is not None:
        a = jnp.where(valid, a, 0.0)
    r_out = r_in + later[:, 0:1] + log_1m[:, 0:1]
    return a, r_out


def _head_rms(o, seg_ref, g):
    ms = _dot((o * o).astype(BF16), seg_ref[...])
    return o * lax.rsqrt(ms + NORM_EPS) * g


def _sb_prompt_kernel(q_ref, k_ref, v_ref, g_ref, tri_ref, seg_ref, o_ref, q_sc, r_sc, acc_sc, *, tq, tk):
    i = pl.program_id(1)
    n_blocks = q_sc.shape[0]
    _stack_halves(q_ref, q_sc, tq)
    r_sc[...] = jnp.zeros(r_sc.shape, F32)
    acc_sc[...] = jnp.zeros(acc_sc.shape, F32)

    def step(j, masked):
        start = pl.multiple_of(j * tk, tk)
        blocks = range(n_blocks)
        lanes = [slice(p * LANES, (p + 1) * LANES) for p in blocks]
        r_old = [r_sc[p] for p in blocks]
        acc_old = [acc_sc[p] for p in blocks]
        z = [_dot(q_sc[p], k_ref[0, j, lanes[p], :]) for p in blocks]
        valid = None
        if masked:
            row = lax.broadcasted_iota(jnp.int32, z[0].shape, 0) % tq
            col = lax.broadcasted_iota(jnp.int32, z[0].shape, 1)
            valid = j * tk + col < i * tq + row
        ar = [_sb_block(z[p], valid, tri_ref, r_old[p]) for p in blocks]
        r_max = jnp.full((1, 1), NEG_BIG, F32)
        for p in blocks:
            acc_sc[p] = acc_old[p] + _dot(ar[p][0].astype(BF16), v_ref[0, pl.ds(start, tk), lanes[p]])
            r_sc[p] = ar[p][1]
            r_max = jnp.maximum(r_max, jnp.max(ar[p][1], axis=0, keepdims=True))
        return (r_max[0, 0] >= SB_DONE).astype(jnp.int32)

    n_full = i * (tq // tk)
    go = jnp.int32(1)
    for d in reversed(range(tq // tk)):
        go = step(n_full + d, True)

    def not_done(c):
        return jnp.logical_and(c[0] >= 0, c[1] > 0)

    def body(c):
        return c[0] - 1, step(c[0], False)

    lax.while_loop(not_done, body, (n_full - 1, go))
    lo_half = _half_masks((tq, LANES))
    for p in range(n_blocks):
        acc = acc_sc[p]
        o = jnp.where(lo_half, acc[0:tq], acc[tq:2 * tq])
        o_ref[0, :, p * LANES:(p + 1) * LANES] = _head_rms(o, seg_ref, g_ref[...]).astype(o_ref.dtype)


def _sb_prompt(q, k, v, g_sb, tri, seg, tq, tk):
    b, s, w = q.shape
    n_blocks = w // LANES
    kern = functools.partial(_sb_prompt_kernel, tq=tq, tk=tk)
    return pl.pallas_call(
        kern,
        grid=(b, s // tq),
        in_specs=[
            pl.BlockSpec((1, tq, w), lambda bi, i: (bi, i, 0)),
            pl.BlockSpec((1,) + k.shape[1:], lambda bi, i: (bi, 0, 0, 0)),
            pl.BlockSpec((1, s, w), lambda bi, i: (bi, 0, 0)),
            pl.BlockSpec((1, LANES), lambda bi, i: (0, 0)),
            pl.BlockSpec(tri.shape, lambda bi, i: (0, 0)),
            pl.BlockSpec(seg.shape, lambda bi, i: (0, 0)),
        ],
        out_specs=pl.BlockSpec((1, tq, w), lambda bi, i: (bi, i, 0)),
        out_shape=jax.ShapeDtypeStruct((b, s, w), BF16),
        scratch_shapes=[
            pltpu.VMEM((n_blocks, 2 * tq, LANES), BF16),
            pltpu.VMEM((n_blocks, 2 * tq, 1), F32),
            pltpu.VMEM((n_blocks, 2 * tq, LANES), F32),
        ],
        compiler_params=_cparams(("arbitrary", "arbitrary")),
        name="sb_prompt",
    )(q, k, v, g_sb, tri, seg)


Q_ROWS = 16


def _page_spec(n_pages, first_of_chunk, r, page_shape):
    def index_map(b, c, pt_ref):
        return (pt_ref[b * n_pages + first_of_chunk(c) + r], 0, 0)
    return pl.BlockSpec((1,) + page_shape, index_map)


def _page_scores(q, k_refs, rows):
    return _dot(q, jnp.concatenate([r[0, rows, :] for r in k_refs], axis=1).astype(BF16))


def _diff_sample_kernel(pt_ref, lam_ref, q_ref, kn_ref, vn_ref, g_ref, *rest, n_pp, t_new, out_scale):
    k_refs, v_refs = rest[:n_pp], rest[n_pp:2 * n_pp]
    o_ref, m_sc, l_sc, acc_sc = rest[2 * n_pp:]
    c = pl.program_id(1)
    n_heads = q_ref.shape[-1] // LANES

    @pl.when(c == 0)
    def _():
        m_sc[...] = jnp.full(m_sc.shape, NEG_BIG, F32)
        l_sc[...] = jnp.zeros(l_sc.shape, F32)
        acc_sc[...] = jnp.zeros(acc_sc.shape, F32)

    def update(h, s, pv):
        m = m_sc[h]
        m_new = jnp.maximum(m, jnp.max(s, axis=-1, keepdims=True))
        alpha = jnp.exp(m - m_new)
        p = jnp.exp(s - m_new)
        l_sc[h] = alpha * l_sc[h] + jnp.sum(p, axis=-1, keepdims=True)
        acc_sc[h] = alpha * acc_sc[h] + pv(p.astype(BF16))
        m_sc[h] = m_new

    page_size = k_refs[0].shape[-1]
    heads = range(n_heads)
    lanes = [slice(h * LANES, (h + 1) * LANES) for h in heads]
    m_old = [m_sc[h] for h in heads]
    l_old = [l_sc[h] for h in heads]
    acc_old = [acc_sc[h] for h in heads]
    s = [_page_scores(q_ref[0, :, lanes[h]], k_refs, lanes[h]) for h in heads]
    m_new = [jnp.maximum(m_old[h], jnp.max(s[h], axis=-1, keepdims=True)) for h in heads]
    p = [jnp.exp(s[h] - m_new[h]) for h in heads]
    alpha = [jnp.exp(m_old[h] - m_new[h]) for h in heads]
    for h in heads:
        vb = jnp.concatenate([v_r[0, pl.ds(h, page_size, stride=n_heads), :] for v_r in v_refs], axis=0)
        pv = _dot(p[h].astype(BF16), vb.astype(BF16))
        m_sc[h] = m_new[h]
        l_sc[h] = alpha[h] * l_old[h] + jnp.sum(p[h], axis=-1, keepdims=True)
        acc_sc[h] = alpha[h] * acc_old[h] + pv

    @pl.when(c == pl.num_programs(1) - 1)
    def _():
        lam = lam_ref[0, 0]
        row = lax.broadcasted_iota(jnp.int32, (Q_ROWS, Q_ROWS), 0)
        col = lax.broadcasted_iota(jnp.int32, (Q_ROWS, Q_ROWS), 1)
        valid = jnp.logical_and(col < t_new, col <= row % t_new)
        for h in range(n_heads):
            ls = slice(h * LANES, (h + 1) * LANES)
            s = jnp.where(valid, _dot_t(q_ref[0, :, ls], kn_ref[0, :, ls]), NEG_BIG)
            update(h, s, lambda p, ls=ls: _dot(p, vn_ref[0, :, ls]))
            o_maps = acc_sc[h] / l_sc[h]
            o = o_maps[0:8] - lam * o_maps[t_new:t_new + 8]
            ms = jnp.mean(o * o, axis=-1, keepdims=True)
            o_ref[0, :, ls] = o * lax.rsqrt(ms + NORM_EPS) * g_ref[...] * out_scale


def _diff_sample(pt_flat, lam, q16, k_new, v_new, g_subln, cache_k, cache_v, n_pages, n_pp, t_new, out_scale):
    nb, _, w = q16.shape
    n_heads = w // LANES
    per_seq = lambda b, c, pt: (b, 0, 0)
    chunk0 = lambda c: c * n_pp
    pages = lambda cache: [_page_spec(n_pages, chunk0, r, cache.shape[1:]) for r in range(n_pp)]
    kern = functools.partial(_diff_sample_kernel, n_pp=n_pp, t_new=t_new, out_scale=out_scale)
    grid_spec = pltpu.PrefetchScalarGridSpec(
        num_scalar_prefetch=1,
        grid=(nb, n_pages // n_pp),
        in_specs=[
            pl.BlockSpec(memory_space=pltpu.SMEM),
            pl.BlockSpec((1, Q_ROWS, w), per_seq),
            pl.BlockSpec((1, Q_ROWS, w), per_seq),
            pl.BlockSpec((1, Q_ROWS, w), per_seq),
            pl.BlockSpec((1, LANES), lambda b, c, pt: (0, 0)),
        ] + pages(cache_k) + pages(cache_v),
        out_specs=pl.BlockSpec((1, 8, w), per_seq),
        scratch_shapes=[
            pltpu.VMEM((n_heads, Q_ROWS, 1), F32),
            pltpu.VMEM((n_heads, Q_ROWS, 1), F32),
            pltpu.VMEM((n_heads, Q_ROWS, LANES), F32),
        ],
    )
    return pl.pallas_call(
        kern,
        grid_spec=grid_spec,
        out_shape=jax.ShapeDtypeStruct((nb, 8, w), F32),
        compiler_params=_cparams(("arbitrary", "arbitrary")),
        name="diff_sample",
    )(pt_flat, lam, q16, k_new, v_new, g_subln, *([cache_k] * n_pp), *([cache_v] * n_pp))


def _sb_sample_kernel(pt_ref, q_ref, kn_ref, vn_ref, g_ref, tri_ref, trin_ref, seg_ref, acc_in, r_in, *rest,
                      n_pp, t_new, with_new):
    k_refs, v_refs = rest[:n_pp], rest[n_pp:2 * n_pp]
    o_ref, acc_ref, r_ref = rest[2 * n_pp:]
    c = pl.program_id(1)
    n_blocks = q_ref.shape[-1] // LANES

    blocks = range(n_blocks)
    lanes = [slice(p * LANES, (p + 1) * LANES) for p in blocks]
    if with_new:
        row = lax.broadcasted_iota(jnp.int32, (Q_ROWS, Q_ROWS), 0)
        col = lax.broadcasted_iota(jnp.int32, (Q_ROWS, Q_ROWS), 1)
        valid = jnp.logical_and(col < t_new, col < row % t_new)
        r_old, acc_old = [], []
        for p in blocks:
            z_new = _dot_t(q_ref[0, :, lanes[p]], kn_ref[0, :, lanes[p]])
            a, r = _sb_block(z_new, valid, trin_ref, jnp.zeros((Q_ROWS, 1), F32))
            r_old.append(r)
            acc_old.append(_dot(a.astype(BF16), vn_ref[0, :, lanes[p]]))
    else:
        @pl.when(c == 0)
        def _():
            acc_ref[...] = acc_in[...]
            r_ref[...] = r_in[...]

        r_old = [r_ref[0, :, p * LANES:p * LANES + 1] for p in blocks]
        acc_old = [acc_ref[0, :, lanes[p]] for p in blocks]

    z = [_page_scores(q_ref[0, :, lanes[p]], k_refs, lanes[p]) for p in blocks]
    ar = [_sb_block(z[p], None, tri_ref, r_old[p]) for p in blocks]
    acc_new = []
    for p in blocks:
        vt = jnp.concatenate([v_r[0, lanes[p], :] for v_r in v_refs], axis=1).astype(BF16)
        acc_new.append(acc_old[p] + _dot_t(ar[p][0].astype(BF16), vt))
        acc_ref[0, :, lanes[p]] = acc_new[p]
        r_ref[0, :, lanes[p]] = jnp.broadcast_to(ar[p][1], (Q_ROWS, LANES))

    def finalize(accs):
        lo_half = _half_masks((8, LANES))
        for p in blocks:
            o = jnp.where(lo_half, accs[p][0:8], accs[p][t_new:t_new + 8])
            o_ref[0, :, lanes[p]] = _head_rms(o, seg_ref, g_ref[...])

    if with_new:
        finalize(acc_new)
    else:
        @pl.when(c == pl.num_programs(1) - 1)
        def _():
            finalize([acc_ref[0, :, lanes[p]] for p in blocks])


def _sb_sample(pt_flat, q16, k_new, v_new, g_sb, tri, tri_new, seg, acc_in, r_in, cache_k, cache_v,
               n_pages, page_hi, n_chunks, n_pp, t_new, with_new):
    assert not with_new or n_chunks == 1
    nb, _, w = q16.shape
    per_seq = lambda b, c, pt: (b, 0, 0)
    fixed = lambda b, c, pt: (0, 0)
    first = lambda c: page_hi - (c + 1) * n_pp
    pages = lambda cache: [_page_spec(n_pages, first, r, cache.shape[1:]) for r in range(n_pp)]
    kern = functools.partial(_sb_sample_kernel, n_pp=n_pp, t_new=t_new, with_new=with_new)
    state_spec = pl.BlockSpec((1, Q_ROWS, w), per_seq)
    grid_spec = pltpu.PrefetchScalarGridSpec(
        num_scalar_prefetch=1,
        grid=(nb, n_chunks),
        in_specs=[
            state_spec, state_spec, state_spec,
            pl.BlockSpec((1, LANES), fixed),
            pl.BlockSpec(tri.shape, fixed),
            pl.BlockSpec(tri_new.shape, fixed),
            pl.BlockSpec(seg.shape, fixed),
            state_spec, state_spec,
        ] + pages(cache_k) + pages(cache_v),
        out_specs=[pl.BlockSpec((1, 8, w), per_seq), state_spec, state_spec],
    )
    state = jax.ShapeDtypeStruct((nb, Q_ROWS, w), F32)
    return pl.pallas_call(
        kern,
        grid_spec=grid_spec,
        out_shape=[jax.ShapeDtypeStruct((nb, 8, w), F32), state, state],
        compiler_params=_cparams(("arbitrary", "arbitrary")),
        name="sb_sample_new" if with_new else "sb_sample_rest",
    )(pt_flat, q16, k_new, v_new, g_sb, tri, tri_new, seg, acc_in, r_in,
      *([cache_k] * n_pp), *([cache_v] * n_pp))


def _pack_bf16_pairs(x):
    c = x.shape[-1] // 2
    bits = lax.bitcast_convert_type(x.astype(F32), jnp.uint32)
    return lax.bitcast_convert_type((bits[:, :c] >> 16) | (bits[:, c:] & jnp.uint32(0xFFFF0000)), F32)


def _unpack_bf16_pairs(words):
    w = lax.bitcast_convert_type(words, jnp.uint32)
    lo = lax.bitcast_convert_type(w << 16, F32)
    hi = lax.bitcast_convert_type(w & jnp.uint32(0xFFFF0000), F32)
    return jnp.concatenate([lo, hi], axis=1).astype(BF16)


def _finish_kernel(x_ref, od_ref, os_ref, wo_ref, g_ref, wr_hi_ref, wr_lo_ref, br_ref,
                   h_ref, hn_ref, route_ref):
    half = od_ref.shape[-1]
    h = x_ref[...] + _dot(od_ref[...], wo_ref[0:half, :]) + _dot(os_ref[...], wo_ref[half:, :])
    h_ref[...] = h
    ms = jnp.mean(h * h, axis=-1, keepdims=True)
    hn = h * lax.rsqrt(ms + NORM_EPS) * g_ref[...]
    hn_hi, hn_lo = _split_bf16(hn)
    hn_ref[...] = _pack_bf16_pairs(hn_hi)

    logits = (_dot(hn_hi, wr_hi_ref[...]) + _dot(hn_lo, wr_hi_ref[...]) + _dot(hn_hi, wr_lo_ref[...])
              + br_ref[...])
    n_exp = N_GROUPS * EXPERTS_PER_GROUP
    lane = lax.broadcasted_iota(jnp.int32, logits.shape, 1)

    def first_argmax(v, vmax):
        return jnp.min(jnp.where(v == vmax, lane, LANES), axis=-1, keepdims=True)

    lg = jnp.where(jnp.logical_and(lane >= n_exp, lane < n_exp + N_GROUPS), logits, NEG_BIG)
    mg = jnp.max(lg, axis=-1, keepdims=True)
    grp = first_argmax(lg, mg) - n_exp
    p_top = 1.0 / jnp.sum(jnp.exp(lg - mg), axis=-1, keepdims=True)

    le = jnp.where(jnp.logical_and(lane < n_exp, lane // EXPERTS_PER_GROUP == grp), logits, NEG_BIG)
    m1 = jnp.max(le, axis=-1, keepdims=True)
    i1 = first_argmax(le, m1)
    le2 = jnp.where(lane == i1, NEG_BIG, le)
    m2 = jnp.max(le2, axis=-1, keepdims=True)
    i2 = first_argmax(le2, m2)
    e2 = jnp.exp(m2 - m1)
    gate1 = p_top / (1.0 + e2)
    gate2 = p_top * e2 / (1.0 + e2)
    route = jnp.where(lane == 0, i1.astype(F32),
                      jnp.where(lane == 1, i2.astype(F32),
                                jnp.where(lane == 2, gate1, jnp.where(lane == 3, gate2, 0.0))))
    route_ref[...] = route


def _finish(x2d, od, osb, w_o_bf, g_ffn, wr_hi, wr_lo, b_r, tm):
    n, d = x2d.shape
    half = od.shape[-1]
    row = lambda i: (i, 0)
    fixed = lambda i: (0, 0)
    return pl.pallas_call(
        _finish_kernel,
        grid=(n // tm,),
        in_specs=[
            pl.BlockSpec((tm, d), row),
            pl.BlockSpec((tm, half), row),
            pl.BlockSpec((tm, half), row),
            pl.BlockSpec(w_o_bf.shape, fixed),
            pl.BlockSpec((1, d), fixed),
            pl.BlockSpec(wr_hi.shape, fixed),
            pl.BlockSpec(wr_lo.shape, fixed),
            pl.BlockSpec((1, LANES), fixed),
        ],
        out_specs=[pl.BlockSpec((tm, d), row), pl.BlockSpec((tm, d // 2), row), pl.BlockSpec((tm, LANES), row)],
        out_shape=[jax.ShapeDtypeStruct((n, d), F32), jax.ShapeDtypeStruct((n, d // 2), F32),
                   jax.ShapeDtypeStruct((n, LANES), F32)],
        compiler_params=_cparams(("arbitrary",)),
        name="out_proj_router",
    )(x2d, od, osb, w_o_bf, g_ffn, wr_hi, wr_lo, b_r)


def _expert_kernel(be_ref, nact_ref, x_ref, wg_ref, wu_ref, wd_ref, y_ref, wg_sc, wu_sc, wd_sc):
    blk = pl.program_id(0)
    active = blk < nact_ref[0]

    @pl.when(jnp.logical_and(active, jnp.logical_or(blk == 0, be_ref[blk] != be_ref[jnp.maximum(blk - 1, 0)])))
    def _():
        wg_sc[...] = wg_ref[0].astype(BF16)
        wu_sc[...] = wu_ref[0].astype(BF16)
        wd_sc[...] = wd_ref[0].astype(BF16)

    @pl.when(active)
    def _():
        x = _unpack_bf16_pairs(x_ref[...])
        gate = _dot(x, wg_sc[...])
        up = _dot(x, wu_sc[...])
        hidden = (gate * jax.nn.sigmoid(gate) * up).astype(BF16)
        y_ref[...] = _dot(hidden, wd_sc[...])

    @pl.when(blk >= nact_ref[0])
    def _():
        y_ref[...] = jnp.zeros(y_ref.shape, y_ref.dtype)


def _expert_ffn(block_exp, n_active, xs, w_gate, w_up, w_down, bm):
    n_slots = xs.shape[0]
    d, ff = w_gate.shape[-2:]
    grid_spec = pltpu.PrefetchScalarGridSpec(
        num_scalar_prefetch=2,
        grid=(n_slots // bm,),
        in_specs=[
            pl.BlockSpec((bm, xs.shape[1]), lambda i, be, na: (i, 0)),
            pl.BlockSpec((1, d, ff), lambda i, be, na: (be[i], 0, 0)),
            pl.BlockSpec((1, d, ff), lambda i, be, na: (be[i], 0, 0)),
            pl.BlockSpec((1, ff, d), lambda i, be, na: (be[i], 0, 0)),
        ],
        out_specs=pl.BlockSpec((bm, d), lambda i, be, na: (i, 0)),
        scratch_shapes=[pltpu.VMEM((d, ff), BF16), pltpu.VMEM((d, ff), BF16), pltpu.VMEM((ff, d), BF16)],
    )
    return pl.pallas_call(
        _expert_kernel,
        grid_spec=grid_spec,
        out_shape=jax.ShapeDtypeStruct((n_slots, d), F32),
        compiler_params=_cparams(("arbitrary",)),
        name="expert_ffn",
    )(block_exp, n_active, xs, w_gate, w_up, w_down)


def _moe(h_parts, hn_packed, route, w_gate, w_up, w_down, bm):
    n_tok = hn_packed.shape[0]
    n_exp = w_gate.shape[0]
    ids = route[:, 0:2].astype(jnp.int32)
    gates = route[:, 2:4]
    n_assign = 2 * n_tok
    flat_e = ids.reshape(-1)
    experts = jnp.arange(n_exp, dtype=jnp.int32)
    order = jnp.argsort(flat_e, stable=True).astype(jnp.int32)
    rank = jnp.argsort(order).astype(jnp.int32)
    counts = jnp.sum((flat_e[:, None] == experts[None, :]).astype(jnp.int32), axis=0)
    padded = (counts + bm - 1) // bm * bm
    start = jnp.cumsum(counts) - counts
    pend = jnp.cumsum(padded)
    pstart = pend - padded
    dest = (pstart[flat_e] + rank - start[flat_e]).reshape(n_tok, 2)
    n_blocks = -(-n_assign // bm) + n_exp
    blk_start = jnp.arange(n_blocks, dtype=jnp.int32) * bm
    block_exp = jnp.minimum(jnp.sum((blk_start[:, None] >= pend[None, :]).astype(jnp.int32), axis=1), n_exp - 1)
    n_active = (pend[-1:] // bm).astype(jnp.int32)
    offset = (blk_start - pstart[block_exp])[:, None] + jnp.arange(bm, dtype=jnp.int32)[None, :]
    src = jnp.minimum(start[block_exp][:, None] + offset, n_assign - 1)
    slot_tok = jnp.where(offset < counts[block_exp][:, None], order[src] // 2, 0).reshape(-1)
    xs = hn_packed[slot_tok]
    yb = _expert_ffn(block_exp, n_active, xs, w_gate, w_up, w_down, bm)
    outs, lo = [], 0
    for h in h_parts:
        hi = lo + h.shape[0]
        outs.append(h + (gates[lo:hi, 0:1] * yb[dest[lo:hi, 0]] + gates[lo:hi, 1:2] * yb[dest[lo:hi, 1]]))
        lo = hi
    return outs


def _block_diag_mean(width, group):
    idx = jnp.arange(width) // group
    return jnp.where(idx[:, None] == idx[None, :], 1.0 / group, 0.0).astype(BF16)


def _tri(n):
    idx = jnp.arange(n)
    tri = (idx[:, None] > idx[None, :]).astype(BF16)
    return jnp.concatenate([tri, tri], axis=0)


def _sample_rows(a, nb, t):
    w = a.shape[-1]
    a = a.reshape(nb, t, w)
    lo = (jnp.arange(w) % LANES) < DH
    zeros = jnp.zeros((nb, Q_ROWS - 2 * t, w), a.dtype)
    return jnp.concatenate([jnp.where(lo, a, 0), jnp.where(lo, 0, a), zeros], axis=1)


def _pad_rows(a, nb, t):
    w = a.shape[-1]
    a = a.reshape(nb, t, w)
    return jnp.concatenate([a, jnp.zeros((nb, Q_ROWS - t, w), a.dtype)], axis=1)


def kernel(x_prompt, x_sample, cache_k_diff, cache_v_diff, cache_k_sb, cache_v_sb, page_table, g_attn_norm, w_in, g_q_norm, g_k_norm, lambda_q1, lambda_k1, lambda_q2, lambda_k2, g_subln, g_sb_out, w_o, g_ffn_norm, w_router_group, b_router_group, w_router_expert, b_router_expert, w_exp_gate, w_exp_up, w_exp_down):
    depth = w_in.shape[0]
    b, s, d = x_prompt.shape
    nb, t_new, _ = x_sample.shape
    n_phys, page_size = cache_k_diff.shape[1], cache_k_diff.shape[2]
    n_pages = page_table.shape[1]
    past_len = n_pages * page_size
    width = w_in.shape[-1] // 6
    n_exp = w_exp_gate.shape[1]

    tm = min(256, nb * t_new)
    tq, tk = min(PROMPT_TQ, s), min(PROMPT_TK, s)
    n_pp_diff = min(32, n_pages)
    n_pp_sb = min(2, n_pages)
    bm = 256

    seg64 = _block_diag_mean(width, DH)
    seg64_lane = _block_diag_mean(LANES, DH)
    tri_p = _tri(tk)
    tri_s = _tri(n_pp_sb * page_size)
    tri_new = _tri(Q_ROWS)
    pos_p = _rope_tables(jnp.arange(s), 1)
    pos_s = _rope_tables(past_len + jnp.arange(t_new), tm // t_new)

    hp = x_prompt.reshape(b * s, d)
    hs = x_sample.reshape(nb * t_new, d)
    outs = [[] for _ in range(8)]
    for layer in range(depth):
        lam_init = 0.8 - 0.6 * math.exp(-0.3 * layer)
        lam = (jnp.exp(jnp.sum(lambda_q1[layer].astype(F32) * lambda_k1[layer].astype(F32)))
               - jnp.exp(jnp.sum(lambda_q2[layer].astype(F32) * lambda_k2[layer].astype(F32))) + lam_init)
        lam = lam.reshape(1, 1).astype(F32)
        out_scale = 1.0 - lam_init
        g_attn = g_attn_norm[layer].reshape(1, d)
        w_in_bf = w_in[layer].astype(BF16)
        gq = jnp.tile(g_q_norm[layer], width // DH).reshape(1, width)
        gk = jnp.tile(g_k_norm[layer], width // DH).reshape(1, width)
        g_sub = g_subln[layer].reshape(1, LANES)
        g_sb = jnp.tile(g_sb_out[layer], LANES // DH).reshape(1, LANES)

        (kd_p, vd_p, ks_p, vs_p, qd_pb, kd_pb, vd_pb, qs_pb, ks_pb, vs_pb) = _project(
            hp, pos_p, g_attn, w_in_bf, gq, gk, seg64, tk, seq_len=s)
        (kd_s, vd_s, ks_s, vs_s, qd_sb, kd_sb, vd_sb, qs_sb, ks_sb, vs_sb) = _project(
            hs, pos_s, g_attn, w_in_bf, gq, gk, seg64, tm)

        r3 = lambda a: a.reshape(b, s, width)
        qd3, kd3, vd3 = r3(qd_pb), kd_pb, r3(vd_pb)
        qmax = _normed_head_bound(g_q_norm[layer]) * (DH ** -0.5)
        kmax = _normed_head_bound(g_k_norm[layer])
        od_p = lax.cond(
            qmax * kmax <= BOUND_MAX,
            lambda: _diff_prompt_bounded(lam, kmax.reshape(1, 1), qd3, kd3, vd3, g_sub, out_scale, tq, tk),
            lambda: _diff_prompt(lam, qd3, kd3, vd3, g_sub, out_scale, tq, tk))
        osb_p = _sb_prompt(r3(qs_pb), ks_pb, r3(vs_pb), g_sb, tri_p, seg64_lane, tq, tk)

        pt_flat = (page_table.astype(jnp.int32) + layer * n_phys).reshape(-1)
        transposed_pages = lambda c: jnp.transpose(c, (0, 1, 3, 4, 2)).reshape(depth * n_phys, width, page_size)
        v_diff_pages = cache_v_diff.reshape(depth * n_phys, page_size * (width // DV_DIFF_W), DV_DIFF_W)
        od_s = _diff_sample(pt_flat, lam, _sample_rows(qd_sb, nb, t_new), _pad_rows(kd_sb, nb, t_new),
                            _pad_rows(vd_sb, nb, t_new), g_sub, transposed_pages(cache_k_diff),
                            v_diff_pages, n_pages, n_pp_diff, t_new, out_scale)
        q16s = _sample_rows(qs_sb, nb, t_new)
        kn16, vn16 = _pad_rows(ks_sb, nb, t_new), _pad_rows(vs_sb, nb, t_new)
        cks, cvs = transposed_pages(cache_k_sb), transposed_pages(cache_v_sb)
        zeros_state = jnp.zeros((nb, Q_ROWS, width), F32)
        sb_args = (pt_flat, q16s, kn16, vn16, g_sb, tri_s, tri_new, seg64_lane)
        osb_s, acc_s, r_s = _sb_sample(*sb_args, zeros_state, zeros_state, cks, cvs,
                                       n_pages, n_pages, 1, n_pp_sb, t_new, True)
        n_rest = (n_pages - n_pp_sb) // n_pp_sb
        if n_rest > 0:
            unfinished = jnp.max(r_s[:, :2 * t_new, :]) >= SB_DONE
            osb_s = lax.cond(
                unfinished,
                lambda: _sb_sample(*sb_args, acc_s, r_s, cks, cvs, n_pages, n_pages - n_pp_sb,
                                   n_rest, n_pp_sb, t_new, False)[0],
                lambda: osb_s)

        w_o_bf = w_o[layer].astype(BF16)
        g_ffn = g_ffn_norm[layer].reshape(1, d)
        w_r = jnp.zeros((d, LANES), F32)
        w_r = w_r.at[:, :n_exp].set(w_router_expert[layer]).at[:, n_exp:n_exp + N_GROUPS].set(w_router_group[layer])
        wr_hi, wr_lo = _split_bf16(w_r)
        b_r = jnp.zeros((1, LANES), F32)
        b_r = b_r.at[0, :n_exp].set(b_router_expert[layer]).at[0, n_exp:n_exp + N_GROUPS].set(b_router_group[layer])

        od_s2 = od_s[:, :t_new].reshape(nb * t_new, width).astype(BF16)
        osb_s2 = osb_s[:, :t_new].reshape(nb * t_new, width).astype(BF16)
        fin = lambda xx, od, osb, tile: _finish(xx, od, osb, w_o_bf, g_ffn, wr_hi, wr_lo, b_r, tile)
        h_p, hn_p, route_p = fin(hp, od_p.reshape(b * s, width), osb_p.reshape(b * s, width), min(256, s))
        h_s, hn_s, route_s = fin(hs, od_s2, osb_s2, tm)

        hp, hs = _moe((h_p, h_s), jnp.concatenate([hn_p, hn_s]), jnp.concatenate([route_p, route_s]),
                      w_exp_gate[layer], w_exp_up[layer], w_exp_down[layer], bm)

        heads_last = lambda a: jnp.transpose(a.reshape(b, width // DH, DH, s), (0, 3, 1, 2))
        for lst, val in zip(outs, (heads_last(kd_p), vd_p, heads_last(ks_p), heads_last(vs_p),
                                   kd_s, vd_s, ks_s, vs_s)):
            lst.append(val)

    h_diff, h_sb = width // DV_DIFF_W, width // DH
    shapes_p = [(b, s, 2 * h_diff, DH), (b, s, h_diff, DV_DIFF_W), (b, s, h_sb, DH), (b, s, h_sb, DH)]
    shapes_s = [(nb, t_new) + sh[2:] for sh in shapes_p]
    stacked = [jnp.stack([v.reshape(sh) for v in lst]) for lst, sh in zip(outs, shapes_p + shapes_s)]
    return (hp.reshape(b, s, d), hs.reshape(nb, t_new, d), *stacked)
```
